```python
import jax
import jax.numpy as jnp
from jax import lax
import numpy as np

D_MODEL = 1024
BATCH = 8
SEQ = 4096
DEPTH = 2

HEAD_DIM = 64
RET_W = 3 * D_MODEL // 8
RET_HEADS = RET_W // HEAD_DIM
HGRN_W = 3 * D_MODEL // 8
HGRN_HEADS = HGRN_W // HEAD_DIM
HGRN_DK = HEAD_DIM
GLA_W = D_MODEL // 4
GLA_HEADS = GLA_W // HEAD_DIM
GLA_DK = HEAD_DIM // 2
GLA_QK = GLA_HEADS * GLA_DK
GLA_GATE_RANK = 16
GLA_TAU = 16.0
D_MIX = RET_W + HGRN_W + GLA_W
CHUNK = 64
ROPE_BASE = 10000.0
EPS = 1e-6
F_MIN = 1e-30
N_GROUPS = 4
EXPERTS_PER_GROUP = 8
N_EXPERTS = N_GROUPS * EXPERTS_PER_GROUP
TOP_K_INNER = 2
D_EXPERT = D_MODEL // 4
N_MOD = 6
IN_PROJ_SIZES = (RET_W,) * 4 + (HGRN_W,) * 4 + (GLA_QK, GLA_QK, GLA_W, GLA_W, GLA_GATE_RANK)
IN_PROJ_DIM = sum(IN_PROJ_SIZES)

kernel_name = "hybrid_ret_hgrn2_gla_hmoe_adaln"


def rmsnorm(x, gain):
    xf = x.astype(jnp.float32)
    y = xf * lax.rsqrt(jnp.mean(xf * xf, axis=-1, keepdims=True) + EPS)
    return y.astype(x.dtype) * gain


def split_heads(t, n_heads):
    b, s, w = t.shape
    return t.reshape(b, s, n_heads, w // n_heads).transpose(0, 2, 1, 3).astype(jnp.float32)


def head_rmsnorm(o, gain):
    b, h, s, d = o.shape
    o = o.transpose(0, 2, 1, 3)
    o = o * lax.rsqrt(jnp.mean(o * o, axis=-1, keepdims=True) + EPS)
    return o.reshape(b, s, h * d) * gain.astype(jnp.float32)


def rope_tables(positions, dim):
    inv_freq = ROPE_BASE ** (-jnp.arange(0, dim, 2, dtype=jnp.float32) / dim)
    ang = positions.astype(jnp.float32)[..., None] * inv_freq
    return jnp.cos(ang)[:, None], jnp.sin(ang)[:, None]


def apply_rope(t, cos, sin):
    t1, t2 = jnp.split(t, 2, axis=-1)
    return jnp.concatenate([t1 * cos - t2 * sin, t1 * sin + t2 * cos], axis=-1)


def retention_chunkwise(q, k, v):
    b, h, s, dk = q.shape
    dv = v.shape[-1]
    n = s // CHUNK
    log_gamma = jnp.log1p(-jnp.exp2(-5.0 - jnp.arange(h, dtype=jnp.float32)))
    idx = jnp.arange(CHUNK, dtype=jnp.float32)
    rel = idx[:, None] - idx[None, :]
    decay = jnp.where(rel >= 0, jnp.exp(log_gamma[:, None, None] * jnp.maximum(rel, 0.0)), 0.0)
    zeta = jnp.exp(log_gamma[:, None] * (CHUNK - 1 - idx))
    xi = jnp.exp(log_gamma[:, None] * (idx + 1.0))
    chunk_decay = jnp.exp(log_gamma * CHUNK)[:, None, None]
    qc = q.reshape(b, h, n, CHUNK, dk)
    kc = k.reshape(b, h, n, CHUNK, dk)
    vc = v.reshape(b, h, n, CHUNK, dv)
    scores = jnp.einsum('bhnid,bhnjd->bhnij', qc, kc) * decay[:, None]
    o_intra = jnp.einsum('bhnij,bhnje->bhnie', scores, vc)
    u = jnp.einsum('bhnjd,hj,bhnje->nbhde', kc, zeta, vc)

    def step(r, u_i):
        return r * chunk_decay + u_i, r

    _, r_prev = lax.scan(step, jnp.zeros((b, h, dk, dv), jnp.float32), u)
    o_inter = jnp.einsum('bhnid,nbhde,hi->bhnie', qc, r_prev, xi)
    return (o_intra + o_inter).reshape(b, h, s, dv)


def gated_linear_recurrence_chunkwise(q, k, v, log_a):
    b, h, s, dk = q.shape
    dv = v.shape[-1]
    n = s // CHUNK
    mask = jnp.tril(jnp.ones((CHUNK, CHUNK), dtype=bool))[:, :, None]

    def to_chunks(t):
        return jnp.moveaxis(t.reshape(b, h, n, CHUNK, t.shape[-1]), 2, 0)

    def step(state, inp):
        qc, kc, vc, lac = inp
        cum = jnp.cumsum(lac, axis=-2)
        o_inter = jnp.einsum('bhcd,bhde->bhce', qc * jnp.exp(cum), state)
        diff = cum[:, :, :, None, :] - cum[:, :, None, :, :]
        dec = jnp.where(mask, jnp.exp(jnp.where(mask, diff, 0.0)), 0.0)
        scores = jnp.einsum('bhtd,bhsd,bhtsd->bhts', qc, kc, dec)
        o = o_inter + jnp.einsum('bhts,bhse->bhte', scores, vc)
        cum_last = cum[:, :, -1:, :]
        new_state = (jnp.exp(cum_last[:, :, 0, :])[..., None] * state
                     + jnp.einsum('bhsd,bhse->bhde', kc * jnp.exp(cum_last - cum), vc))
        return new_state, o

    s0 = jnp.zeros((b, h, dk, dv), jnp.float32)
    _, o = lax.scan(step, s0, (to_chunks(q), to_chunks(k), to_chunks(v), to_chunks(log_a)))
    return jnp.moveaxis(o, 0, 2).reshape(b, h, s, dv)


def hybrid_mixer(h, cos, sin, w_in, ret_norm, hgrn_norm, hgrn_lb, gla_wa2, gla_ba, gla_norm, w_out):
    proj = h @ w_in
    splits = np.cumsum(IN_PROJ_SIZES)[:-1].tolist()
    (rq, rk, rv, rg, hq, hf, hi, hg, gq, gk, gv, gg, ga) = jnp.split(proj, splits, axis=-1)

    q = apply_rope(split_heads(rq, RET_HEADS), cos, sin)
    k = apply_rope(split_heads(rk, RET_HEADS), cos, sin) * HEAD_DIM ** -0.5
    o_ret = retention_chunkwise(q, k, split_heads(rv, RET_HEADS))
    o_ret = head_rmsnorm(o_ret, ret_norm) * jax.nn.silu(rg.astype(jnp.float32))

    lb = hgrn_lb.astype(jnp.float32).reshape(HGRN_HEADS, 1, HGRN_DK)
    z = split_heads(hf, HGRN_HEADS)
    f = lb + (1.0 - lb) * jax.nn.sigmoid(z)
    log_f = jnp.log(jnp.maximum(f, F_MIN))
    k = (1.0 - lb) * jax.nn.sigmoid(-z)
    q = jax.nn.silu(split_heads(hq, HGRN_HEADS)) * HGRN_DK ** -0.5
    o_hgrn = gated_linear_recurrence_chunkwise(q, k, split_heads(hi, HGRN_HEADS), log_f)
    o_hgrn = head_rmsnorm(o_hgrn, hgrn_norm) * jax.nn.silu(hg.astype(jnp.float32))

    log_a = jax.nn.log_sigmoid((ga @ gla_wa2 + gla_ba).astype(jnp.float32)) / GLA_TAU
    q = split_heads(gq, GLA_HEADS) * GLA_DK ** -0.5
    o_gla = gated_linear_recurrence_chunkwise(q, split_heads(gk, GLA_HEADS), split_heads(gv, GLA_HEADS),
                                              split_heads(log_a, GLA_HEADS))
    o_gla = head_rmsnorm(o_gla, gla_norm) * jax.nn.silu(gg.astype(jnp.float32))

    merged = jnp.concatenate([o_ret, o_hgrn, o_gla], axis=-1).astype(h.dtype)
    return merged @ w_out


def hierarchical_moe(h, w_rg, b_rg, w_re, b_re, w_gate, w_up, w_down):
    b, s, d = h.shape
    tok = h.reshape(b * s, d)
    g_logits = (tok @ w_rg + b_rg).astype(jnp.float32)
    g_idx = jnp.argmax(g_logits, axis=-1)
    g_w = jnp.max(jax.nn.softmax(g_logits, axis=-1), axis=-1, keepdims=True)
    e_logits = (tok @ w_re + b_re).astype(jnp.float32).reshape(-1, N_GROUPS, EXPERTS_PER_GROUP)
    e_in_group = jnp.einsum('nge,ng->ne', e_logits, jax.nn.one_hot(g_idx, N_GROUPS, dtype=jnp.float32))
    top_val, top_idx = lax.top_k(e_in_group, TOP_K_INNER)
    top_w = jax.nn.softmax(top_val, axis=-1) * g_w
    expert_ids = g_idx[:, None] * EXPERTS_PER_GROUP + top_idx
    combine = jnp.einsum('nk,nke->ne', top_w,
                         jax.nn.one_hot(expert_ids, N_EXPERTS, dtype=jnp.float32)).astype(h.dtype)
    out = jnp.zeros_like(tok)
    for e in range(N_EXPERTS):
        hid = jax.nn.silu(tok @ w_gate[e]) * (tok @ w_up[e])
        out = out + combine[:, e:e + 1] * (hid @ w_down[e])
    return out.reshape(b, s, d)


def setup_inputs(seed: int = 0) -> dict:
    key = jax.random.key(seed)
    ks = jax.random.split(key, 24)
    f32 = jnp.float32

    def nrm(k, shape, scale):
        return jax.random.normal(k, shape, f32) * scale

    x = nrm(ks[0], (BATCH, SEQ, D_MODEL), 1.0)
    c = nrm(ks[1], (BATCH, D_MODEL), 1.0)
    positions = (jax.random.randint(ks[2], (BATCH, 1), 0, 1024)
                 + jnp.arange(SEQ, dtype=jnp.int32)[None, :]).astype(jnp.int32)
    w_ada = nrm(ks[3], (DEPTH, D_MODEL, N_MOD * D_MODEL), 0.5 * D_MODEL ** -0.5)
    b_ada = nrm(ks[4], (DEPTH, N_MOD * D_MODEL), 0.02)
    norm_mix = 1.0 + nrm(ks[5], (DEPTH, D_MODEL), 0.02)
    norm_ffn = 1.0 + nrm(ks[6], (DEPTH, D_MODEL), 0.02)
    w_in = nrm(ks[7], (DEPTH, D_MODEL, IN_PROJ_DIM), D_MODEL ** -0.5)
    ret_norm = 1.0 + nrm(ks[8], (DEPTH, RET_W), 0.02)
    hgrn_norm = 1.0 + nrm(ks[9], (DEPTH, HGRN_W), 0.02)
    hgrn_lb_logits = nrm(ks[10], (DEPTH, HGRN_HEADS * HGRN_DK), 1.0)
    gla_wa2 = nrm(ks[11], (DEPTH, GLA_GATE_RANK, GLA_QK), GLA_GATE_RANK ** -0.5)
    gla_ba = nrm(ks[12], (DEPTH, GLA_QK), 0.1)
    gla_norm = 1.0 + nrm(ks[13], (DEPTH, GLA_W), 0.02)
    w_out = nrm(ks[14], (DEPTH, D_MIX, D_MODEL), D_MIX ** -0.5)
    router_group_w = nrm(ks[15], (DEPTH, D_MODEL, N_GROUPS), D_MODEL ** -0.5)
    router_group_b = nrm(ks[16], (DEPTH, N_GROUPS), 0.01)
    router_expert_w = nrm(ks[17], (DEPTH, D_MODEL, N_EXPERTS), D_MODEL ** -0.5)
    router_expert_b = nrm(ks[18], (DEPTH, N_EXPERTS), 0.01)
    expert_w_gate = nrm(ks[19], (DEPTH, N_EXPERTS, D_MODEL, D_EXPERT), D_MODEL ** -0.5)
    expert_w_up = nrm(ks[20], (DEPTH, N_EXPERTS, D_MODEL, D_EXPERT), D_MODEL ** -0.5)
    expert_w_down = nrm(ks[21], (DEPTH, N_EXPERTS, D_EXPERT, D_MODEL), D_EXPERT ** -0.5)
    norm_final = 1.0 + nrm(ks[22], (D_MODEL,), 0.02)
    return {"x": x, "c": c, "positions": positions, "w_ada": w_ada, "b_ada": b_ada,
            "norm_mix": norm_mix, "norm_ffn": norm_ffn, "w_in": w_in, "ret_norm": ret_norm,
            "hgrn_norm": hgrn_norm, "hgrn_lb_logits": hgrn_lb_logits, "gla_wa2": gla_wa2,
            "gla_ba": gla_ba, "gla_norm": gla_norm, "w_out": w_out,
            "router_group_w": router_group_w, "router_group_b": router_group_b,
            "router_expert_w": router_expert_w, "router_expert_b": router_expert_b,
            "expert_w_gate": expert_w_gate, "expert_w_up": expert_w_up,
            "expert_w_down": expert_w_down, "norm_final": norm_final}


def reference(x, c, positions, w_ada, b_ada, norm_mix, norm_ffn, w_in, ret_norm, hgrn_norm,
              hgrn_lb_logits, gla_wa2, gla_ba, gla_norm, w_out, router_group_w, router_group_b,
              router_expert_w, router_expert_b, expert_w_gate, expert_w_up, expert_w_down,
              norm_final):
    cos, sin = rope_tables(positions, HEAD_DIM)
    lb_w = jax.nn.softmax(hgrn_lb_logits.astype(jnp.float32), axis=0)
    lower_bounds = jnp.cumsum(lb_w, axis=0) - lb_w[0]
    c_act = jax.nn.silu(c)
    for layer in range(DEPTH):
        mod = c_act @ w_ada[layer] + b_ada[layer]
        shift_m, scale_m, gate_m, shift_f, scale_f, gate_f = [m[:, None, :] for m in jnp.split(mod, N_MOD, axis=-1)]
        h = rmsnorm(x, norm_mix[layer]) * (1.0 + scale_m) + shift_m
        x = x + gate_m * hybrid_mixer(h, cos, sin, w_in[layer], ret_norm[layer], hgrn_norm[layer],
                                      lower_bounds[layer], gla_wa2[layer], gla_ba[layer],
                                      gla_norm[layer], w_out[layer])
        h = rmsnorm(x, norm_ffn[layer]) * (1.0 + scale_f) + shift_f
        x = x + gate_f * hierarchical_moe(h, router_group_w[layer], router_group_b[layer],
                                          router_expert_w[layer], router_expert_b[layer],
                                          expert_w_gate[layer], expert_w_up[layer], expert_w_down[layer])
    return rmsnorm(x, norm_final)
```

```python
import functools

import jax
import jax.numpy as jnp
import numpy as np
from jax import lax
from jax.experimental import pallas as pl
from jax.experimental.pallas import tpu as pltpu

F32 = jnp.float32
BF16 = jnp.bfloat16
HIGHEST = lax.Precision.HIGHEST

D_MODEL = 1024
HEAD_DIM = 64
LANES = 128
RET_W = 384
HGRN_W = 384
GLA_W = 256
GLA_QK = 128
GLA_DK = 32
GLA_RANK = 16
GLA_TAU = 16.0
CHUNK = 64
SUB = 16
ROPE_BASE = 10000.0
EPS = 1e-6
F_MIN = 1e-30
N_GROUPS = 4
EXPERTS_PER_GROUP = 8
N_EXPERTS = 32
D_EXPERT = 256
N_MOD = 6
RET_HEADS = RET_W // HEAD_DIM
N_RET_PAIRS = RET_W // LANES
N_HGRN_PAIRS = HGRN_W // LANES

C_RQ, C_RK, C_RV, C_RG = 0, 384, 768, 1152
C_HQ, C_HF, C_HI, C_HG = 1536, 1920, 2304, 2688
C_GQ, C_GK, C_GV, C_GG, C_GA = 3072, 3200, 3328, 3584, 3840
IN_PROJ_DIM = 3856
PROJ_W = C_GA + LANES

ROW_TILE = 512
MIX_TILE = 256
MOE_TILE = 1024
VMEM_LIMIT = 56 * 1024 * 1024

NT_DIMS = (((1,), (1,)), ((), ()))
TN_DIMS = (((0,), (0,)), ((), ()))


def _silu(x):
    return x * jax.nn.sigmoid(x)


def _params(*sem):
    return pltpu.CompilerParams(dimension_semantics=sem, vmem_limit_bytes=VMEM_LIMIT)


def _ada_kernel(c_ref, w_ref, b_ref, o_ref):
    ca = _silu(c_ref[...])
    o_ref[0] = jnp.dot(ca, w_ref[0], precision=HIGHEST, preferred_element_type=F32) + b_ref[0]


def _ada_mod(c, w_ada, b_ada):
    depth, d, n = w_ada.shape
    b = c.shape[0]
    tn = 1536
    return pl.pallas_call(
        _ada_kernel,
        grid=(depth, n // tn),
        in_specs=[pl.BlockSpec((b, d), lambda l, j: (0, 0)),
                  pl.BlockSpec((1, d, tn), lambda l, j: (l, 0, j)),
                  pl.BlockSpec((1, 1, tn), lambda l, j: (l, 0, j))],
        out_specs=pl.BlockSpec((1, b, tn), lambda l, j: (l, 0, j)),
        out_shape=jax.ShapeDtypeStruct((depth, b, n), F32),
        compiler_params=_params("arbitrary", "arbitrary"),
        name="ada_mod",
    )(c, w_ada, b_ada.reshape(depth, 1, n))


def _modulated_norm(x, gain, shift, scale):
    ms = jnp.mean(x * x, axis=-1, keepdims=True)
    return (x * lax.rsqrt(ms + EPS)) * gain * (1.0 + scale) + shift


def _inproj_kernel(x_ref, mod_ref, g_ref, w_ref, o_ref):
    h = _modulated_norm(x_ref[...], g_ref[...], mod_ref[0, 0:1, :], mod_ref[0, 1:2, :])
    o_ref[...] = jnp.dot(h.astype(BF16), w_ref[...], preferred_element_type=F32)


def _inproj(x2d, mod, gain, w_pad, seq):
    n, d = x2d.shape
    tm = ROW_TILE
    per_b = seq // tm
    return pl.pallas_call(
        _inproj_kernel,
        grid=(n // tm,),
        in_specs=[pl.BlockSpec((tm, d), lambda i: (i, 0)),
                  pl.BlockSpec((1, N_MOD, d), lambda i: (i // per_b, 0, 0)),
                  pl.BlockSpec((1, d), lambda i: (0, 0)),
                  pl.BlockSpec((d, PROJ_W), lambda i: (0, 0))],
        out_specs=pl.BlockSpec((tm, PROJ_W), lambda i: (i, 0)),
        out_shape=jax.ShapeDtypeStruct((n, PROJ_W), F32),
        compiler_params=_params("arbitrary"),
        name="norm_inproj",
    )(x2d, mod, gain, w_pad)


def _head_norm_gate(o, bdmean, gain, g):
    ms = jnp.dot(o * o, bdmean, precision=HIGHEST, preferred_element_type=F32)
    return o * lax.rsqrt(ms + EPS) * gain * _silu(g)


def _gated_chunks(q, k, v, cum, st, key_head_w, bd):
    t = q.shape[0]
    vw = v.shape[1]
    n_heads = LANES // key_head_w
    n_sub = CHUNK // SUB
    klane = lax.broadcasted_iota(jnp.int32, (CHUNK, LANES), 1)
    krow = lax.broadcasted_iota(jnp.int32, (CHUNK, LANES), 0)
    srow = lax.broadcasted_iota(jnp.int32, (n_heads * CHUNK, CHUNK), 0) % CHUNK
    scol = lax.broadcasted_iota(jnp.int32, (n_heads * CHUNK, CHUNK), 1)
    causal = srow >= scol
    outs = []
    for c in range(t // CHUNK):
        r = slice(c * CHUNK, (c + 1) * CHUNK)
        cm = cum[r]
        kc = k[r]
        cl = cm[CHUNK - 1:CHUNK, :]
        bounds = [jnp.zeros((1, LANES), F32)] + [cm[i * SUB - 1:i * SUB, :] for i in range(1, n_sub)]
        bmat = jnp.concatenate([jnp.broadcast_to(bi, (SUB, LANES)) for bi in bounds], axis=0)
        ebmat = jnp.concatenate([jnp.broadcast_to(jnp.exp(bi), (SUB, LANES)) for bi in bounds], axis=0)
        qt = q[r] * jnp.exp(cm - bmat)
        qe = qt * ebmat
        lhs = jnp.concatenate([jnp.where(krow // SUB == i, qt, 0.0) for i in range(n_sub)], axis=1)
        kparts = []
        for i in range(n_sub):
            hi = (i + 1) * SUB
            ki = kc[:hi] * jnp.exp(bounds[i] - cm[:hi])
            if hi < CHUNK:
                ki = jnp.concatenate([ki, jnp.zeros((CHUNK - hi, LANES), F32)], axis=0)
            kparts.append(ki)
        kstack = jnp.concatenate(kparts, axis=1).astype(BF16)
        hlane = jnp.concatenate([klane] * n_sub, axis=1) // key_head_w
        lhs_h = jnp.concatenate([jnp.where(hlane == h, lhs, 0.0) for h in range(n_heads)],
                                axis=0).astype(BF16)
        s = lax.dot_general(lhs_h, kstack, NT_DIMS, preferred_element_type=F32)
        pm = jnp.where(causal, s, 0.0).astype(BF16)
        vb = v[r].astype(BF16)
        pv = jnp.dot(pm, vb, preferred_element_type=F32)
        groups = []
        for gi in range(vw // LANES):
            h0 = gi * (LANES // HEAD_DIM)
            top = pv[h0 * CHUNK:(h0 + 1) * CHUNK, gi * LANES:(gi + 1) * LANES]
            bot = pv[(h0 + 1) * CHUNK:(h0 + 2) * CHUNK, gi * LANES:(gi + 1) * LANES]
            groups.append(jnp.where(klane < HEAD_DIM, top, bot))
        intra = groups[0] if len(groups) == 1 else jnp.concatenate(groups, axis=1)
        o = lax.dot_general(qe.astype(BF16), st.astype(BF16), NT_DIMS, preferred_element_type=F32)
        outs.append(o + intra)
        kl = (kc * jnp.exp(cl - cm)).astype(BF16)
        upd = lax.dot_general(vb, kl, TN_DIMS, preferred_element_type=F32)
        st = st * jnp.exp(cl) + jnp.where(bd, upd, 0.0)
    return jnp.concatenate(outs, axis=0), st


def _mixer_kernel(proj_ref, cos_ref, sin_ref, tri_ref, rdec_ref, xi_ref, zeta_ref, rcd_ref,
                  retn_ref, hgn_ref, lb_ref, glan_ref, ba_ref, wa2_ref,
                  o_ref, ret_st, hg_st, gl_st):
    @pl.when(pl.program_id(1) == 0)
    def _():
        ret_st[...] = jnp.zeros_like(ret_st)
        hg_st[...] = jnp.zeros_like(hg_st)
        gl_st[...] = jnp.zeros_like(gl_st)

    t = proj_ref.shape[1]
    lane = lax.broadcasted_iota(jnp.int32, (t, LANES), 1)
    sq_r = lax.broadcasted_iota(jnp.int32, (LANES, LANES), 0)
    sq_c = lax.broadcasted_iota(jnp.int32, (LANES, LANES), 1)
    bd = (sq_r // HEAD_DIM) == (sq_c // HEAD_DIM)
    bdmean = jnp.where(bd, 1.0 / HEAD_DIM, 0.0).astype(F32)
    tri = tri_ref[...]

    def cols(c0, w=LANES):
        return proj_ref[0, :, c0:c0 + w]

    cosb = cos_ref[0]
    sinb = sin_ref[0]
    first_half = (lane % HEAD_DIM) < (HEAD_DIM // 2)

    def rope(x):
        swapped = jnp.where(first_half, pltpu.roll(x, LANES - HEAD_DIM // 2, 1),
                            pltpu.roll(x, HEAD_DIM // 2, 1))
        return x * cosb + swapped * sinb

    for p in range(N_RET_PAIRS):
        o0 = p * LANES
        q = rope(cols(C_RQ + o0))
        k = rope(cols(C_RK + o0)) * (HEAD_DIM ** -0.5)
        kb = k.astype(BF16)
        vb = cols(C_RV + o0).astype(BF16)
        st = ret_st[p]
        o = lax.dot_general((q * xi_ref[p]).astype(BF16), st.astype(BF16), NT_DIMS,
                            preferred_element_type=F32)
        intra = None
        for hh in range(2):
            qm = jnp.where(lane // HEAD_DIM == hh, q, 0.0).astype(BF16)
            s = lax.dot_general(qm, kb, NT_DIMS, preferred_element_type=F32)
            pm = (s * rdec_ref[2 * p + hh]).astype(BF16)
            oh = jnp.dot(pm, vb, preferred_element_type=F32)
            intra = oh if intra is None else jnp.where(lane < HEAD_DIM, intra, oh)
        o = o + intra
        upd = lax.dot_general(vb, (k * zeta_ref[p]).astype(BF16), TN_DIMS, preferred_element_type=F32)
        ret_st[p] = st * rcd_ref[p] + jnp.where(bd, upd, 0.0)
        y = _head_norm_gate(o, bdmean, retn_ref[:, o0:o0 + LANES], cols(C_RG + o0))
        o_ref[0, :, o0:o0 + LANES] = y.astype(o_ref.dtype)

    for p in range(N_HGRN_PAIRS):
        o0 = p * LANES
        lb = lb_ref[:, o0:o0 + LANES]
        z = cols(C_HF + o0)
        f = lb + (1.0 - lb) * jax.nn.sigmoid(z)
        cum = jnp.dot(tri, jnp.log(jnp.maximum(f, F_MIN)), precision=HIGHEST, preferred_element_type=F32)
        k = (1.0 - lb) * jax.nn.sigmoid(-z)
        q = _silu(cols(C_HQ + o0)) * (HEAD_DIM ** -0.5)
        o, st = _gated_chunks(q, k, cols(C_HI + o0), cum, hg_st[p], HEAD_DIM, bd)
        hg_st[p] = st
        y = _head_norm_gate(o, bdmean, hgn_ref[:, o0:o0 + LANES], cols(C_HG + o0))
        o_ref[0, :, RET_W + o0:RET_W + o0 + LANES] = y.astype(o_ref.dtype)

    u = jnp.dot(cols(C_GA), wa2_ref[...], precision=HIGHEST, preferred_element_type=F32) + ba_ref[...]
    log_a = (jnp.minimum(u, 0.0) - jnp.log(1.0 + jnp.exp(-jnp.abs(u)))) * (1.0 / GLA_TAU)
    cum = jnp.dot(tri, log_a, precision=HIGHEST, preferred_element_type=F32)
    ge = lax.broadcasted_iota(jnp.int32, (GLA_W, LANES), 0)
    gd = lax.broadcasted_iota(jnp.int32, (GLA_W, LANES), 1)
    bdg = (ge // HEAD_DIM) == (gd // GLA_DK)
    q = cols(C_GQ) * (GLA_DK ** -0.5)
    o, st = _gated_chunks(q, cols(C_GK), cols(C_GV, GLA_W), cum, gl_st[...], GLA_DK, bdg)
    gl_st[...] = st
    for gi in range(GLA_W // LANES):
        o0 = gi * LANES
        y = _head_norm_gate(o[:, o0:o0 + LANES], bdmean, glan_ref[:, o0:o0 + LANES], cols(C_GG + o0))
        o_ref[0, :, RET_W + HGRN_W + o0:RET_W + HGRN_W + o0 + LANES] = y.astype(o_ref.dtype)


def _mixer_constants(t):
    idx = np.arange(t)
    tri = ((idx[:, None] >= idx[None, :]) & (idx[:, None] // CHUNK == idx[None, :] // CHUNK))
    log_gamma = jnp.log1p(-jnp.exp2(-5.0 - jnp.arange(RET_HEADS, dtype=F32)))
    i = jnp.arange(t, dtype=F32)
    rel = i[:, None] - i[None, :]
    rdec = jnp.where(rel >= 0, jnp.exp(log_gamma[:, None, None] * jnp.maximum(rel, 0.0)), 0.0)
    lane_head = jnp.repeat(log_gamma, HEAD_DIM).reshape(N_RET_PAIRS, 1, LANES)
    xi = jnp.exp(lane_head * (i[None, :, None] + 1.0))
    zeta = jnp.exp(lane_head * (t - 1.0 - i[None, :, None]))
    rcd = jnp.exp(lane_head * float(t))
    return jnp.asarray(tri, F32), rdec, xi, zeta, rcd


def _mixer(proj, cos_t, sin_t, consts, retn, hgn, lb, glan, ba, wa2p):
    b, s, _ = proj.shape
    t = MIX_TILE
    tri, rdec, xi, zeta, rcd = consts
    const2 = lambda bi, j: (0, 0)
    const3 = lambda bi, j: (0, 0, 0)
    return pl.pallas_call(
        _mixer_kernel,
        grid=(b, s // t),
        in_specs=[pl.BlockSpec((1, t, PROJ_W), lambda bi, j: (bi, j, 0)),
                  pl.BlockSpec((1, t, LANES), lambda bi, j: (bi, j, 0)),
                  pl.BlockSpec((1, t, LANES), lambda bi, j: (bi, j, 0)),
                  pl.BlockSpec((t, t), const2),
                  pl.BlockSpec((RET_HEADS, t, t), const3),
                  pl.BlockSpec((N_RET_PAIRS, t, LANES), const3),
                  pl.BlockSpec((N_RET_PAIRS, t, LANES), const3),
                  pl.BlockSpec((N_RET_PAIRS, 1, LANES), const3),
                  pl.BlockSpec((1, RET_W), const2),
                  pl.BlockSpec((1, HGRN_W), const2),
                  pl.BlockSpec((1, HGRN_W), const2),
                  pl.BlockSpec((1, GLA_W), const2),
                  pl.BlockSpec((1, GLA_QK), const2),
                  pl.BlockSpec((LANES, GLA_QK), const2)],
        out_specs=pl.BlockSpec((1, t, D_MODEL), lambda bi, j: (bi, j, 0)),
        out_shape=jax.ShapeDtypeStruct((b, s, D_MODEL), BF16),
        scratch_shapes=[pltpu.VMEM((N_RET_PAIRS, LANES, LANES), F32),
                        pltpu.VMEM((N_HGRN_PAIRS, LANES, LANES), F32),
                        pltpu.VMEM((GLA_W, LANES), F32)],
        compiler_params=_params("arbitrary", "arbitrary"),
        name="mixer",
    )(proj, cos_t, sin_t, tri, rdec, xi, zeta, rcd, retn, hgn, lb, glan, ba, wa2p)


def _route(logits):
    lane = lax.broadcasted_iota(jnp.int32, logits.shape, 1)
    neg = jnp.float32(-jnp.inf)
    big = jnp.int32(LANES)
    is_g = (lane >= N_EXPERTS) & (lane < N_EXPERTS + N_GROUPS)
    gl = jnp.where(is_g, logits, neg)
    gmax = jnp.max(gl, axis=-1, keepdims=True)
    gidx = jnp.min(jnp.where(gl == gmax, lane - N_EXPERTS, big), axis=-1, keepdims=True)
    g_w = 1.0 / jnp.sum(jnp.where(is_g, jnp.exp(gl - gmax), 0.0), axis=-1, keepdims=True)
    in_grp = (lane < N_EXPERTS) & ((lane // EXPERTS_PER_GROUP) == gidx)
    el = jnp.where(in_grp, logits, neg)
    v1 = jnp.max(el, axis=-1, keepdims=True)
    i1 = jnp.min(jnp.where(in_grp & (el == v1), lane, big), axis=-1, keepdims=True)
    rest = in_grp & (lane != i1)
    el2 = jnp.where(rest, logits, neg)
    v2 = jnp.max(el2, axis=-1, keepdims=True)
    i2 = jnp.min(jnp.where(rest & (el2 == v2), lane, big), axis=-1, keepdims=True)
    e2 = jnp.exp(v2 - v1)
    w1 = g_w / (1.0 + e2)
    w2 = g_w * e2 / (1.0 + e2)
    return jnp.where(lane == i1, w1, jnp.where(lane == i2, w2, 0.0))


def _outproj_kernel(m_ref, x_ref, mod_ref, g_ref, wo_ref, wr_ref, br_ref, x1_ref, h2_ref, comb_ref):
    att = jnp.dot(m_ref[...], wo_ref[...], preferred_element_type=F32)
    x1 = x_ref[...] + mod_ref[0, 2:3, :] * att
    x1_ref[...] = x1
    h2 = _modulated_norm(x1, g_ref[...], mod_ref[0, 3:4, :], mod_ref[0, 4:5, :])
    h2_ref[...] = h2.astype(BF16)
    logits = jnp.dot(h2, wr_ref[...], precision=HIGHEST, preferred_element_type=F32) + br_ref[...]
    comb_ref[...] = _route(logits)


def _outproj(merged2d, x2d, mod, gain, w_out, w_r, b_r, seq):
    n, d = x2d.shape
    tm = ROW_TILE
    per_b = seq // tm
    row = lambda i: (i, 0)
    const2 = lambda i: (0, 0)
    return pl.pallas_call(
        _outproj_kernel,
        grid=(n // tm,),
        in_specs=[pl.BlockSpec((tm, d), row),
                  pl.BlockSpec((tm, d), row),
                  pl.BlockSpec((1, N_MOD, d), lambda i: (i // per_b, 0, 0)),
                  pl.BlockSpec((1, d), const2),
                  pl.BlockSpec((d, d), const2),
                  pl.BlockSpec((d, LANES), const2),
                  pl.BlockSpec((1, LANES), const2)],
        out_specs=[pl.BlockSpec((tm, d), row),
                   pl.BlockSpec((tm, d), row),
                   pl.BlockSpec((tm, LANES), row)],
        out_shape=[jax.ShapeDtypeStruct((n, d), F32),
                   jax.ShapeDtypeStruct((n, d), BF16),
                   jax.ShapeDtypeStruct((n, LANES), F32)],
        compiler_params=_params("arbitrary"),
        name="outproj_route",
    )(merged2d, x2d, mod, gain, w_out, w_r, b_r)


def _moe_kernel(h2_ref, comb_ref, x1_ref, mod_ref, wg_ref, wu_ref, wd_ref, nf_ref, o_ref, acc_ref,
                *, final):
    e = pl.program_id(1)

    @pl.when(e == 0)
    def _():
        acc_ref[...] = jnp.zeros_like(acc_ref)

    h = h2_ref[...]
    a = jnp.dot(h, wg_ref[0], preferred_element_type=F32)
    b = jnp.dot(h, wu_ref[0], preferred_element_type=F32)
    y = jnp.dot((_silu(a) * b).astype(BF16), wd_ref[0], preferred_element_type=F32)
    comb = comb_ref[...]
    lane = lax.broadcasted_iota(jnp.int32, comb.shape, 1)
    cw = jnp.sum(jnp.where(lane == e, comb, 0.0), axis=-1, keepdims=True)
    acc_ref[...] += cw * y

    @pl.when(e == pl.num_programs(1) - 1)
    def _():
        x2 = x1_ref[...] + mod_ref[0, 5:6, :] * acc_ref[...]
        if final:
            ms = jnp.mean(x2 * x2, axis=-1, keepdims=True)
            x2 = x2 * lax.rsqrt(ms + EPS) * nf_ref[...]
        o_ref[...] = x2


def _moe(h2, comb, x1, mod, wg, wu, wd, nf, seq, final):
    n, d = x1.shape
    tm = min(MOE_TILE, seq)
    per_b = seq // tm
    row = lambda i, e: (i, 0)
    return pl.pallas_call(
        functools.partial(_moe_kernel, final=final),
        grid=(n // tm, N_EXPERTS),
        in_specs=[pl.BlockSpec((tm, d), row),
                  pl.BlockSpec((tm, LANES), row),
                  pl.BlockSpec((tm, d), row),
                  pl.BlockSpec((1, N_MOD, d), lambda i, e: (i // per_b, 0, 0)),
                  pl.BlockSpec((1, d, D_EXPERT), lambda i, e: (e, 0, 0)),
                  pl.BlockSpec((1, d, D_EXPERT), lambda i, e: (e, 0, 0)),
                  pl.BlockSpec((1, D_EXPERT, d), lambda i, e: (e, 0, 0)),
                  pl.BlockSpec((1, d), lambda i, e: (0, 0))],
        out_specs=pl.BlockSpec((tm, d), row),
        out_shape=jax.ShapeDtypeStruct((n, d), F32),
        scratch_shapes=[pltpu.VMEM((tm, d), F32)],
        compiler_params=_params("arbitrary", "arbitrary"),
        name="moe_experts",
    )(h2, comb, x1, mod, wg, wu, wd, nf)


def kernel(x, c, positions, w_ada, b_ada, norm_mix, norm_ffn, w_in, ret_norm, hgrn_norm, hgrn_lb_logits,
           gla_wa2, gla_ba, gla_norm, w_out, router_group_w, router_group_b, router_expert_w,
           router_expert_b, expert_w_gate, expert_w_up, expert_w_down, norm_final):
    b, s, d = x.shape
    depth = w_ada.shape[0]
    n = b * s
    assert d == D_MODEL and s % ROW_TILE == 0 and s % MIX_TILE == 0

    inv_freq = ROPE_BASE ** (-jnp.arange(0, HEAD_DIM, 2, dtype=F32) / HEAD_DIM)
    ang = positions.astype(F32)[..., None] * inv_freq
    cos, sin = jnp.cos(ang), jnp.sin(ang)
    cos_t = jnp.concatenate([cos, cos, cos, cos], axis=-1)
    sin_t = jnp.concatenate([-sin, sin, -sin, sin], axis=-1)

    lb_w = jax.nn.softmax(hgrn_lb_logits.astype(F32), axis=0)
    lower_bounds = jnp.cumsum(lb_w, axis=0) - lb_w[0]

    consts = _mixer_constants(MIX_TILE)
    mod_all = _ada_mod(c, w_ada, b_ada).reshape(depth, b, N_MOD, d)

    w_in_pad = jnp.pad(w_in, ((0, 0), (0, 0), (0, PROJ_W - IN_PROJ_DIM))).astype(BF16)
    wa2_pad = jnp.pad(gla_wa2, ((0, 0), (0, LANES - GLA_RANK), (0, 0)))
    w_out_b = w_out.astype(BF16)
    pad_r = LANES - N_EXPERTS - N_GROUPS
    w_r = jnp.pad(jnp.concatenate([router_expert_w, router_group_w], axis=-1), ((0, 0), (0, 0), (0, pad_r)))
    b_r = jnp.pad(jnp.concatenate([router_expert_b, router_group_b], axis=-1), ((0, 0), (0, pad_r)))
    wg_b = expert_w_gate.astype(BF16)
    wu_b = expert_w_up.astype(BF16)
    wd_b = expert_w_down.astype(BF16)

    x2d = x.reshape(n, d)
    for l in range(depth):
        mod = mod_all[l]
        proj = _inproj(x2d, mod, norm_mix[l].reshape(1, d), w_in_pad[l], s)
        merged = _mixer(proj.reshape(b, s, PROJ_W), cos_t, sin_t, consts,
                        ret_norm[l].reshape(1, RET_W), hgrn_norm[l].reshape(1, HGRN_W),
                        lower_bounds[l].reshape(1, HGRN_W), gla_norm[l].reshape(1, GLA_W),
                        gla_ba[l].reshape(1, GLA_QK), wa2_pad[l])
        x1, h2, comb = _outproj(merged.reshape(n, d), x2d, mod, norm_ffn[l].reshape(1, d), w_out_b[l],
                                w_r[l], b_r[l].reshape(1, LANES), s)
        x2d = _moe(h2, comb, x1, mod, wg_b[l], wu_b[l], wd_b[l], norm_final.reshape(1, d), s,
                   final=(l == depth - 1))
    return x2d.reshape(b, s, d)
```

```python
import functools

import jax
import jax.numpy as jnp
import numpy as np
from jax import lax
from jax.experimental import pallas as pl
from jax.experimental.pallas import tpu as pltpu

F32 = jnp.float32
BF16 = jnp.bfloat16
HIGHEST = lax.Precision.HIGHEST

D_MODEL = 1024
HEAD_DIM = 64
LANES = 128
RET_W = 384
HGRN_W = 384
GLA_W = 256
GLA_QK = 128
GLA_DK = 32
GLA_RANK = 16
GLA_TAU = 16.0
CHUNK = 64
SUB = 16
ROPE_BASE = 10000.0
EPS = 1e-6
F_MIN = 1e-30
N_GROUPS = 4
EXPERTS_PER_GROUP = 8
N_EXPERTS = 32
D_EXPERT = 256
N_MOD = 6
RET_HEADS = RET_W // HEAD_DIM
N_RET_PAIRS = RET_W // LANES
N_HGRN_PAIRS = HGRN_W // LANES

C_RQ, C_RK, C_RV, C_RG = 0, 384, 768, 1152
C_HQ, C_HF, C_HI, C_HG = 1536, 1920, 2304, 2688
C_GQ, C_GK, C_GV, C_GG, C_GA = 3072, 3200, 3328, 3584, 3840
IN_PROJ_DIM = 3856
PROJ_W = C_GA + LANES

ROW_TILE = 512
MIX_TILE = 256
EXPERT_TILE = 256
DISPATCH_TILE = 512
COMBINE_TILE = 256
VMEM_LIMIT = 56 * 1024 * 1024

NT_DIMS = (((1,), (1,)), ((), ()))
TN_DIMS = (((0,), (0,)), ((), ()))


def _silu(x):
    return x * jax.nn.sigmoid(x)


def _params(*sem):
    return pltpu.CompilerParams(dimension_semantics=sem, vmem_limit_bytes=VMEM_LIMIT)


def _ada_kernel(c_ref, w_ref, b_ref, o_ref):
    ca = _silu(c_ref[...])
    o_ref[0] = jnp.dot(ca, w_ref[0], precision=HIGHEST, preferred_element_type=F32) + b_ref[0]


def _ada_mod(c, w_ada, b_ada):
    depth, d, n = w_ada.shape
    b = c.shape[0]
    tn = 1536
    return pl.pallas_call(
        _ada_kernel,
        grid=(depth, n // tn),
        in_specs=[pl.BlockSpec((b, d), lambda l, j: (0, 0)),
                  pl.BlockSpec((1, d, tn), lambda l, j: (l, 0, j)),
                  pl.BlockSpec((1, 1, tn), lambda l, j: (l, 0, j))],
        out_specs=pl.BlockSpec((1, b, tn), lambda l, j: (l, 0, j)),
        out_shape=jax.ShapeDtypeStruct((depth, b, n), F32),
        compiler_params=_params("arbitrary", "arbitrary"),
        name="ada_mod",
    )(c, w_ada, b_ada.reshape(depth, 1, n))


def _modulated_norm(x, gain, shift, scale):
    ms = jnp.mean(x * x, axis=-1, keepdims=True)
    return (x * lax.rsqrt(ms + EPS)) * gain * (1.0 + scale) + shift


def _inproj_kernel(x_ref, mod_ref, g_ref, w_ref, o_ref):
    h = _modulated_norm(x_ref[...], g_ref[...], mod_ref[0, 0:1, :], mod_ref[0, 1:2, :])
    o_ref[...] = jnp.dot(h.astype(BF16), w_ref[...], preferred_element_type=F32)


def _inproj(x2d, mod, gain, w_pad, seq):
    n, d = x2d.shape
    tm = ROW_TILE
    per_b = seq // tm
    return pl.pallas_call(
        _inproj_kernel,
        grid=(n // tm,),
        in_specs=[pl.BlockSpec((tm, d), lambda i: (i, 0)),
                  pl.BlockSpec((1, N_MOD, d), lambda i: (i // per_b, 0, 0)),
                  pl.BlockSpec((1, d), lambda i: (0, 0)),
                  pl.BlockSpec((d, PROJ_W), lambda i: (0, 0))],
        out_specs=pl.BlockSpec((tm, PROJ_W), lambda i: (i, 0)),
        out_shape=jax.ShapeDtypeStruct((n, PROJ_W), F32),
        compiler_params=_params("arbitrary"),
        name="norm_inproj",
    )(x2d, mod, gain, w_pad)


def _head_norm_gate(o, bdmean, gain, g):
    ms = jnp.dot(o * o, bdmean, precision=HIGHEST, preferred_element_type=F32)
    return o * lax.rsqrt(ms + EPS) * gain * _silu(g)


def _gated_chunks(q, k, v, cum, st, key_head_w, bd):
    t = q.shape[0]
    vw = v.shape[1]
    n_heads = LANES // key_head_w
    n_sub = CHUNK // SUB
    klane = lax.broadcasted_iota(jnp.int32, (CHUNK, LANES), 1)
    krow = lax.broadcasted_iota(jnp.int32, (CHUNK, LANES), 0)
    srow = lax.broadcasted_iota(jnp.int32, (n_heads * CHUNK, CHUNK), 0) % CHUNK
    scol = lax.broadcasted_iota(jnp.int32, (n_heads * CHUNK, CHUNK), 1)
    causal = srow >= scol
    outs = []
    for c in range(t // CHUNK):
        r = slice(c * CHUNK, (c + 1) * CHUNK)
        cm = cum[r]
        kc = k[r]
        cl = cm[CHUNK - 1:CHUNK, :]
        bounds = [jnp.zeros((1, LANES), F32)] + [cm[i * SUB - 1:i * SUB, :] for i in range(1, n_sub)]
        bmat = jnp.concatenate([jnp.broadcast_to(bi, (SUB, LANES)) for bi in bounds], axis=0)
        ebmat = jnp.concatenate([jnp.broadcast_to(jnp.exp(bi), (SUB, LANES)) for bi in bounds], axis=0)
        qt = q[r] * jnp.exp(cm - bmat)
        qe = qt * ebmat
        lhs = jnp.concatenate([jnp.where(krow // SUB == i, qt, 0.0) for i in range(n_sub)], axis=1)
        kparts = []
        for i in range(n_sub):
            hi = (i + 1) * SUB
            ki = kc[:hi] * jnp.exp(bounds[i] - cm[:hi])
            if hi < CHUNK:
                ki = jnp.concatenate([ki, jnp.zeros((CHUNK - hi, LANES), F32)], axis=0)
            kparts.append(ki)
        kstack = jnp.concatenate(kparts, axis=1).astype(BF16)
        hlane = jnp.concatenate([klane] * n_sub, axis=1) // key_head_w
        lhs_h = jnp.concatenate([jnp.where(hlane == h, lhs, 0.0) for h in range(n_heads)],
                                axis=0).astype(BF16)
        s = lax.dot_general(lhs_h, kstack, NT_DIMS, preferred_element_type=F32)
        pm = jnp.where(causal, s, 0.0).astype(BF16)
        vb = v[r].astype(BF16)
        pv = jnp.dot(pm, vb, preferred_element_type=F32)
        groups = []
        for gi in range(vw // LANES):
            h0 = gi * (LANES // HEAD_DIM)
            top = pv[h0 * CHUNK:(h0 + 1) * CHUNK, gi * LANES:(gi + 1) * LANES]
            bot = pv[(h0 + 1) * CHUNK:(h0 + 2) * CHUNK, gi * LANES:(gi + 1) * LANES]
            groups.append(jnp.where(klane < HEAD_DIM, top, bot))
        intra = groups[0] if len(groups) == 1 else jnp.concatenate(groups, axis=1)
        o = lax.dot_general(qe.astype(BF16), st.astype(BF16), NT_DIMS, preferred_element_type=F32)
        outs.append(o + intra)
        kl = (kc * jnp.exp(cl - cm)).astype(BF16)
        upd = lax.dot_general(vb, kl, TN_DIMS, preferred_element_type=F32)
        st = st * jnp.exp(cl) + jnp.where(bd, upd, 0.0)
    return jnp.concatenate(outs, axis=0), st


def _mixer_kernel(proj_ref, cos_ref, sin_ref, tri_ref, rdec_ref, xi_ref, zeta_ref, rcd_ref,
                  retn_ref, hgn_ref, lb_ref, glan_ref, ba_ref, wa2_ref,
                  o_ref, ret_st, hg_st, gl_st):
    @pl.when(pl.program_id(1) == 0)
    def _():
        ret_st[...] = jnp.zeros_like(ret_st)
        hg_st[...] = jnp.zeros_like(hg_st)
        gl_st[...] = jnp.zeros_like(gl_st)

    t = proj_ref.shape[1]
    lane = lax.broadcasted_iota(jnp.int32, (t, LANES), 1)
    sq_r = lax.broadcasted_iota(jnp.int32, (LANES, LANES), 0)
    sq_c = lax.broadcasted_iota(jnp.int32, (LANES, LANES), 1)
    bd = (sq_r // HEAD_DIM) == (sq_c // HEAD_DIM)
    bdmean = jnp.where(bd, 1.0 / HEAD_DIM, 0.0).astype(F32)
    tri = tri_ref[...]

    def cols(c0, w=LANES):
        return proj_ref[0, :, c0:c0 + w]

    cosb = cos_ref[0]
    sinb = sin_ref[0]
    first_half = (lane % HEAD_DIM) < (HEAD_DIM // 2)

    def rope(x):
        swapped = jnp.where(first_half, pltpu.roll(x, LANES - HEAD_DIM // 2, 1),
                            pltpu.roll(x, HEAD_DIM // 2, 1))
        return x * cosb + swapped * sinb

    for p in range(N_RET_PAIRS):
        o0 = p * LANES
        q = rope(cols(C_RQ + o0))
        k = rope(cols(C_RK + o0)) * (HEAD_DIM ** -0.5)
        kb = k.astype(BF16)
        vb = cols(C_RV + o0).astype(BF16)
        st = ret_st[p]
        o = lax.dot_general((q * xi_ref[p]).astype(BF16), st.astype(BF16), NT_DIMS,
                            preferred_element_type=F32)
        intra = None
        for hh in range(2):
            qm = jnp.where(lane // HEAD_DIM == hh, q, 0.0).astype(BF16)
            s = lax.dot_general(qm, kb, NT_DIMS, preferred_element_type=F32)
            pm = (s * rdec_ref[2 * p + hh]).astype(BF16)
            oh = jnp.dot(pm, vb, preferred_element_type=F32)
            intra = oh if intra is None else jnp.where(lane < HEAD_DIM, intra, oh)
        o = o + intra
        upd = lax.dot_general(vb, (k * zeta_ref[p]).astype(BF16), TN_DIMS, preferred_element_type=F32)
        ret_st[p] = st * rcd_ref[p] + jnp.where(bd, upd, 0.0)
        y = _head_norm_gate(o, bdmean, retn_ref[:, o0:o0 + LANES], cols(C_RG + o0))
        o_ref[0, :, o0:o0 + LANES] = y.astype(o_ref.dtype)

    for p in range(N_HGRN_PAIRS):
        o0 = p * LANES
        lb = lb_ref[:, o0:o0 + LANES]
        z = cols(C_HF + o0)
        f = lb + (1.0 - lb) * jax.nn.sigmoid(z)
        cum = jnp.dot(tri, jnp.log(jnp.maximum(f, F_MIN)), precision=HIGHEST, preferred_element_type=F32)
        k = (1.0 - lb) * jax.nn.sigmoid(-z)
        q = _silu(cols(C_HQ + o0)) * (HEAD_DIM ** -0.5)
        o, st = _gated_chunks(q, k, cols(C_HI + o0), cum, hg_st[p], HEAD_DIM, bd)
        hg_st[p] = st
        y = _head_norm_gate(o, bdmean, hgn_ref[:, o0:o0 + LANES], cols(C_HG + o0))
        o_ref[0, :, RET_W + o0:RET_W + o0 + LANES] = y.astype(o_ref.dtype)

    u = jnp.dot(cols(C_GA), wa2_ref[...], precision=HIGHEST, preferred_element_type=F32) + ba_ref[...]
    log_a = (jnp.minimum(u, 0.0) - jnp.log(1.0 + jnp.exp(-jnp.abs(u)))) * (1.0 / GLA_TAU)
    cum = jnp.dot(tri, log_a, precision=HIGHEST, preferred_element_type=F32)
    ge = lax.broadcasted_iota(jnp.int32, (GLA_W, LANES), 0)
    gd = lax.broadcasted_iota(jnp.int32, (GLA_W, LANES), 1)
    bdg = (ge // HEAD_DIM) == (gd // GLA_DK)
    q = cols(C_GQ) * (GLA_DK ** -0.5)
    o, st = _gated_chunks(q, cols(C_GK), cols(C_GV, GLA_W), cum, gl_st[...], GLA_DK, bdg)
    gl_st[...] = st
    for gi in range(GLA_W // LANES):
        o0 = gi * LANES
        y = _head_norm_gate(o[:, o0:o0 + LANES], bdmean, glan_ref[:, o0:o0 + LANES], cols(C_GG + o0))
        o_ref[0, :, RET_W + HGRN_W + o0:RET_W + HGRN_W + o0 + LANES] = y.astype(o_ref.dtype)


def _mixer_constants(t):
    idx = np.arange(t)
    tri = ((idx[:, None] >= idx[None, :]) & (idx[:, None] // CHUNK == idx[None, :] // CHUNK))
    log_gamma = jnp.log1p(-jnp.exp2(-5.0 - jnp.arange(RET_HEADS, dtype=F32)))
    i = jnp.arange(t, dtype=F32)
    rel = i[:, None] - i[None, :]
    rdec = jnp.where(rel >= 0, jnp.exp(log_gamma[:, None, None] * jnp.maximum(rel, 0.0)), 0.0)
    lane_head = jnp.repeat(log_gamma, HEAD_DIM).reshape(N_RET_PAIRS, 1, LANES)
    xi = jnp.exp(lane_head * (i[None, :, None] + 1.0))
    zeta = jnp.exp(lane_head * (t - 1.0 - i[None, :, None]))
    rcd = jnp.exp(lane_head * float(t))
    return jnp.asarray(tri, F32), rdec, xi, zeta, rcd


def _mixer(proj, cos_t, sin_t, consts, retn, hgn, lb, glan, ba, wa2p):
    b, s, _ = proj.shape
    t = MIX_TILE
    tri, rdec, xi, zeta, rcd = consts
    const2 = lambda bi, j: (0, 0)
    const3 = lambda bi, j: (0, 0, 0)
    return pl.pallas_call(
        _mixer_kernel,
        grid=(b, s // t),
        in_specs=[pl.BlockSpec((1, t, PROJ_W), lambda bi, j: (bi, j, 0)),
                  pl.BlockSpec((1, t, LANES), lambda bi, j: (bi, j, 0)),
                  pl.BlockSpec((1, t, LANES), lambda bi, j: (bi, j, 0)),
                  pl.BlockSpec((t, t), const2),
                  pl.BlockSpec((RET_HEADS, t, t), const3),
                  pl.BlockSpec((N_RET_PAIRS, t, LANES), const3),
                  pl.BlockSpec((N_RET_PAIRS, t, LANES), const3),
                  pl.BlockSpec((N_RET_PAIRS, 1, LANES), const3),
                  pl.BlockSpec((1, RET_W), const2),
                  pl.BlockSpec((1, HGRN_W), const2),
                  pl.BlockSpec((1, HGRN_W), const2),
                  pl.BlockSpec((1, GLA_W), const2),
                  pl.BlockSpec((1, GLA_QK), const2),
                  pl.BlockSpec((LANES, GLA_QK), const2)],
        out_specs=pl.BlockSpec((1, t, D_MODEL), lambda bi, j: (bi, j, 0)),
        out_shape=jax.ShapeDtypeStruct((b, s, D_MODEL), BF16),
        scratch_shapes=[pltpu.VMEM((N_RET_PAIRS, LANES, LANES), F32),
                        pltpu.VMEM((N_HGRN_PAIRS, LANES, LANES), F32),
                        pltpu.VMEM((GLA_W, LANES), F32)],
        compiler_params=_params("arbitrary", "arbitrary"),
        name="mixer",
    )(proj, cos_t, sin_t, tri, rdec, xi, zeta, rcd, retn, hgn, lb, glan, ba, wa2p)


def _route(logits):
    lane = lax.broadcasted_iota(jnp.int32, logits.shape, 1)
    neg = jnp.float32(-jnp.inf)
    big = jnp.int32(LANES)
    is_g = (lane >= N_EXPERTS) & (lane < N_EXPERTS + N_GROUPS)
    gl = jnp.where(is_g, logits, neg)
    gmax = jnp.max(gl, axis=-1, keepdims=True)
    gidx = jnp.min(jnp.where(gl == gmax, lane - N_EXPERTS, big), axis=-1, keepdims=True)
    g_w = 1.0 / jnp.sum(jnp.where(is_g, jnp.exp(gl - gmax), 0.0), axis=-1, keepdims=True)
    in_grp = (lane < N_EXPERTS) & ((lane // EXPERTS_PER_GROUP) == gidx)
    el = jnp.where(in_grp, logits, neg)
    v1 = jnp.max(el, axis=-1, keepdims=True)
    i1 = jnp.min(jnp.where(in_grp & (el == v1), lane, big), axis=-1, keepdims=True)
    rest = in_grp & (lane != i1)
    el2 = jnp.where(rest, logits, neg)
    v2 = jnp.max(el2, axis=-1, keepdims=True)
    i2 = jnp.min(jnp.where(rest & (el2 == v2), lane, big), axis=-1, keepdims=True)
    e2 = jnp.exp(v2 - v1)
    w1 = g_w / (1.0 + e2)
    w2 = g_w * e2 / (1.0 + e2)
    return i1, i2, w1, w2


def _pack_bf16_pairs(h):
    half = h.shape[1] // 2
    bits = pltpu.bitcast(h.astype(BF16).astype(F32), jnp.uint32)
    return (bits[:, :half] >> 16) | (bits[:, half:] & jnp.uint32(0xFFFF0000))


def _unpack_bf16_pairs(p):
    lo = pltpu.bitcast(p << 16, F32)
    hi = pltpu.bitcast(p & jnp.uint32(0xFFFF0000), F32)
    return jnp.concatenate([lo, hi], axis=1).astype(BF16)


R_E1, R_E2, R_RANK1, R_RANK2, R_W1, R_W2 = 0, 1, 2, 3, 4, 5


def _outproj_kernel(m_ref, x_ref, mod_ref, g_ref, wo_ref, wr_ref, br_ref, tril_ref,
                    x1_ref, h2_ref, rinfo_ref, counts_ref, carry_ref):
    @pl.when(pl.program_id(0) == 0)
    def _():
        carry_ref[...] = jnp.zeros_like(carry_ref)

    att = jnp.dot(m_ref[...], wo_ref[...], preferred_element_type=F32)
    x1 = x_ref[...] + mod_ref[0, 2:3, :] * att
    x1_ref[...] = x1
    h2 = _modulated_norm(x1, g_ref[...], mod_ref[0, 3:4, :], mod_ref[0, 4:5, :])
    h2_ref[...] = _pack_bf16_pairs(h2)
    logits = jnp.dot(h2, wr_ref[...], precision=HIGHEST, preferred_element_type=F32) + br_ref[...]
    i1, i2, w1, w2 = _route(logits)
    lane = lax.broadcasted_iota(jnp.int32, logits.shape, 1)
    hit1 = lane == i1
    hit2 = lane == i2
    onehot = jnp.where(hit1 | hit2, 1.0, 0.0)
    before = carry_ref[...] + jnp.dot(tril_ref[...], onehot.astype(BF16), preferred_element_type=F32)
    r1 = jnp.sum(jnp.where(hit1, before, 0.0), axis=-1, keepdims=True)
    r2 = jnp.sum(jnp.where(hit2, before, 0.0), axis=-1, keepdims=True)
    carry_ref[...] += jnp.sum(onehot, axis=0, keepdims=True)
    counts_ref[...] = carry_ref[...]
    rec = jnp.zeros(logits.shape, F32)
    for pos, val in ((R_E1, i1.astype(F32)), (R_E2, i2.astype(F32)), (R_RANK1, r1), (R_RANK2, r2),
                     (R_W1, w1), (R_W2, w2)):
        rec = jnp.where(lane == pos, val, rec)
    rinfo_ref[...] = rec


def _outproj(merged2d, x2d, mod, gain, w_out, w_r, b_r, seq):
    n, d = x2d.shape
    tm = ROW_TILE
    per_b = seq // tm
    row = lambda i: (i, 0)
    const2 = lambda i: (0, 0)
    idx = np.arange(tm)
    tril = jnp.asarray(idx[:, None] > idx[None, :], BF16)
    return pl.pallas_call(
        _outproj_kernel,
        grid=(n // tm,),
        in_specs=[pl.BlockSpec((tm, d), row),
                  pl.BlockSpec((tm, d), row),
                  pl.BlockSpec((1, N_MOD, d), lambda i: (i // per_b, 0, 0)),
                  pl.BlockSpec((1, d), const2),
                  pl.BlockSpec((d, d), const2),
                  pl.BlockSpec((d, LANES), const2),
                  pl.BlockSpec((1, LANES), const2),
                  pl.BlockSpec((tm, tm), const2)],
        out_specs=[pl.BlockSpec((tm, d), row),
                   pl.BlockSpec((tm, d // 2), row),
                   pl.BlockSpec((tm, LANES), row),
                   pl.BlockSpec((1, LANES), const2)],
        out_shape=[jax.ShapeDtypeStruct((n, d), F32),
                   jax.ShapeDtypeStruct((n, d // 2), jnp.uint32),
                   jax.ShapeDtypeStruct((n, LANES), F32),
                   jax.ShapeDtypeStruct((1, LANES), F32)],
        scratch_shapes=[pltpu.VMEM((1, LANES), F32)],
        compiler_params=_params("arbitrary"),
        name="outproj_route",
    )(merged2d, x2d, mod, gain, w_out, w_r, b_r, tril)


def _slot_layout(rinfo, counts, n_tiles):
    cnt = counts[0, :N_EXPERTS].astype(jnp.int32)
    padded = (cnt + EXPERT_TILE - 1) // EXPERT_TILE * EXPERT_TILE
    ends = jnp.cumsum(padded)
    starts = ends - padded
    e = rinfo[:, R_E1:R_E2 + 1].astype(jnp.int32)
    rank = rinfo[:, R_RANK1:R_RANK2 + 1].astype(jnp.int32)
    pos = (jnp.take(starts, e) + rank).reshape(-1)
    tile_start = jnp.arange(n_tiles, dtype=jnp.int32) * EXPERT_TILE
    tile_expert = jnp.minimum(jnp.searchsorted(ends, tile_start, side="right"), N_EXPERTS - 1)
    tile_valid = (tile_start < ends[-1]).astype(jnp.int32)
    return pos, tile_expert.astype(jnp.int32), tile_valid


def _row_copy(src_ref, src_row, dst_ref, dst_row, sem):
    return pltpu.make_async_copy(src_ref.at[pl.ds(src_row, 1)], dst_ref.at[pl.ds(dst_row, 1)], sem)


def _dispatch_kernel(pos_ref, h_ref, xs_in_ref, xs_ref, sem):
    del xs_in_ref
    td = h_ref.shape[0]
    base = pl.program_id(0) * td

    def issue(t, carry):
        for k in range(2):
            _row_copy(h_ref, t, xs_ref, pos_ref[2 * (base + t) + k], sem).start()
        return carry

    lax.fori_loop(0, td, issue, 0, unroll=8)

    def drain(t, carry):
        for k in range(2):
            _row_copy(h_ref, 0, xs_ref, 0, sem).wait()
        return carry

    lax.fori_loop(0, td, drain, 0, unroll=8)


def _dispatch(pos, h2p, n_slots):
    n, w = h2p.shape
    td = min(DISPATCH_TILE, n)
    return pl.pallas_call(
        _dispatch_kernel,
        grid_spec=pltpu.PrefetchScalarGridSpec(
            num_scalar_prefetch=1,
            grid=(n // td,),
            in_specs=[pl.BlockSpec((td, w), lambda i, pos: (i, 0)),
                      pl.BlockSpec(memory_space=pl.ANY)],
            out_specs=pl.BlockSpec(memory_space=pl.ANY),
            scratch_shapes=[pltpu.SemaphoreType.DMA(())]),
        out_shape=jax.ShapeDtypeStruct((n_slots, w), h2p.dtype),
        input_output_aliases={2: 0},
        compiler_params=_params("arbitrary"),
        name="moe_dispatch",
    )(pos, h2p, jnp.zeros((n_slots, w), h2p.dtype))


def _expert_kernel(te_ref, tv_ref, xs_ref, wg_ref, wu_ref, wd_ref, ys_ref, wgb, wub, wdb):
    i = pl.program_id(0)
    new_expert = jnp.logical_or(i == 0, te_ref[i] != te_ref[jnp.maximum(i - 1, 0)])

    @pl.when(new_expert)
    def _():
        wgb[...] = wg_ref[0].astype(BF16)
        wub[...] = wu_ref[0].astype(BF16)
        wdb[...] = wd_ref[0].astype(BF16)

    @pl.when(tv_ref[i] != 0)
    def _():
        x = _unpack_bf16_pairs(xs_ref[...])
        a = jnp.dot(x, wgb[...], preferred_element_type=F32)
        b = jnp.dot(x, wub[...], preferred_element_type=F32)
        ys_ref[...] = jnp.dot((_silu(a) * b).astype(BF16), wdb[...], preferred_element_type=F32)

    @pl.when(tv_ref[i] == 0)
    def _():
        ys_ref[...] = jnp.zeros_like(ys_ref)


def _experts(tile_expert, tile_valid, xs, wg, wu, wd):
    n_slots, w = xs.shape
    d = 2 * w
    tm = EXPERT_TILE
    by_expert = lambda i, te, tv: (te[i], 0, 0)
    return pl.pallas_call(
        _expert_kernel,
        grid_spec=pltpu.PrefetchScalarGridSpec(
            num_scalar_prefetch=2,
            grid=(n_slots // tm,),
            in_specs=[pl.BlockSpec((tm, w), lambda i, te, tv: (i, 0)),
                      pl.BlockSpec((1, d, D_EXPERT), by_expert),
                      pl.BlockSpec((1, d, D_EXPERT), by_expert),
                      pl.BlockSpec((1, D_EXPERT, d), by_expert)],
            out_specs=pl.BlockSpec((tm, d), lambda i, te, tv: (i, 0)),
            scratch_shapes=[pltpu.VMEM((d, D_EXPERT), BF16),
                            pltpu.VMEM((d, D_EXPERT), BF16),
                            pltpu.VMEM((D_EXPERT, d), BF16)]),
        out_shape=jax.ShapeDtypeStruct((n_slots, d), F32),
        compiler_params=_params("arbitrary"),
        name="moe_experts",
    )(tile_expert, tile_valid, xs, wg, wu, wd)


def _combine_kernel(pos_ref, x1_ref, rinfo_ref, mod_ref, nf_ref, ys_ref, o_ref, buf, sem, *, final):
    i = pl.program_id(0)
    tc = x1_ref.shape[0]

    def issue(tile, slot):
        base = tile * tc

        def body(t, carry):
            for k in range(2):
                _row_copy(ys_ref, pos_ref[2 * (base + t) + k], buf.at[slot, k], t, sem.at[slot]).start()
            return carry

        lax.fori_loop(0, tc, body, 0, unroll=8)

    @pl.when(i == 0)
    def _():
        issue(0, 0)

    @pl.when(i + 1 < pl.num_programs(0))
    def _():
        issue(i + 1, (i + 1) % 2)

    slot = i % 2

    def drain(t, carry):
        for k in range(2):
            _row_copy(ys_ref, 0, buf.at[slot, k], 0, sem.at[slot]).wait()
        return carry

    lax.fori_loop(0, tc, drain, 0, unroll=8)

    rinfo = rinfo_ref[...]
    y = rinfo[:, R_W1:R_W1 + 1] * buf[slot, 0] + rinfo[:, R_W2:R_W2 + 1] * buf[slot, 1]
    x2 = x1_ref[...] + mod_ref[0, 5:6, :] * y
    if final:
        ms = jnp.mean(x2 * x2, axis=-1, keepdims=True)
        x2 = x2 * lax.rsqrt(ms + EPS) * nf_ref[...]
    o_ref[...] = x2


def _combine(pos, x1, rinfo, mod, nf, ys, seq, final):
    n, d = x1.shape
    tc = COMBINE_TILE
    per_b = seq // tc
    row = lambda i, pos: (i, 0)
    return pl.pallas_call(
        functools.partial(_combine_kernel, final=final),
        grid_spec=pltpu.PrefetchScalarGridSpec(
            num_scalar_prefetch=1,
            grid=(n // tc,),
            in_specs=[pl.BlockSpec((tc, d), row),
                      pl.BlockSpec((tc, LANES), row),
                      pl.BlockSpec((1, N_MOD, d), lambda i, pos: (i // per_b, 0, 0)),
                      pl.BlockSpec((1, d), lambda i, pos: (0, 0)),
                      pl.BlockSpec(memory_space=pl.ANY)],
            out_specs=pl.BlockSpec((tc, d), row),
            scratch_shapes=[pltpu.VMEM((2, 2, tc, d), F32),
                            pltpu.SemaphoreType.DMA((2,))]),
        out_shape=jax.ShapeDtypeStruct((n, d), F32),
        compiler_params=_params("arbitrary"),
        name="moe_combine",
    )(pos, x1, rinfo, mod, nf, ys)


def kernel(x, c, positions, w_ada, b_ada, norm_mix, norm_ffn, w_in, ret_norm, hgrn_norm, hgrn_lb_logits,
           gla_wa2, gla_ba, gla_norm, w_out, router_group_w, router_group_b, router_expert_w,
           router_expert_b, expert_w_gate, expert_w_up, expert_w_down, norm_final):
    b, s, d = x.shape
    depth = w_ada.shape[0]
    n = b * s
    assert d == D_MODEL and s % ROW_TILE == 0 and s % MIX_TILE == 0

    inv_freq = ROPE_BASE ** (-jnp.arange(0, HEAD_DIM, 2, dtype=F32) / HEAD_DIM)
    ang = positions.astype(F32)[..., None] * inv_freq
    cos, sin = jnp.cos(ang), jnp.sin(ang)
    cos_t = jnp.concatenate([cos, cos, cos, cos], axis=-1)
    sin_t = jnp.concatenate([-sin, sin, -sin, sin], axis=-1)

    lb_w = jax.nn.softmax(hgrn_lb_logits.astype(F32), axis=0)
    lower_bounds = jnp.cumsum(lb_w, axis=0) - lb_w[0]

    consts = _mixer_constants(MIX_TILE)
    mod_all = _ada_mod(c, w_ada, b_ada).reshape(depth, b, N_MOD, d)

    w_in_pad = jnp.pad(w_in, ((0, 0), (0, 0), (0, PROJ_W - IN_PROJ_DIM))).astype(BF16)
    wa2_pad = jnp.pad(gla_wa2, ((0, 0), (0, LANES - GLA_RANK), (0, 0)))
    w_out_b = w_out.astype(BF16)
    pad_r = LANES - N_EXPERTS - N_GROUPS
    w_r = jnp.pad(jnp.concatenate([router_expert_w, router_group_w], axis=-1), ((0, 0), (0, 0), (0, pad_r)))
    b_r = jnp.pad(jnp.concatenate([router_expert_b, router_group_b], axis=-1), ((0, 0), (0, pad_r)))
    n_slots = (2 * n // EXPERT_TILE + N_EXPERTS) * EXPERT_TILE

    x2d = x.reshape(n, d)
    for l in range(depth):
        mod = mod_all[l]
        proj = _inproj(x2d, mod, norm_mix[l].reshape(1, d), w_in_pad[l], s)
        merged = _mixer(proj.reshape(b, s, PROJ_W), cos_t, sin_t, consts,
                        ret_norm[l].reshape(1, RET_W), hgrn_norm[l].reshape(1, HGRN_W),
                        lower_bounds[l].reshape(1, HGRN_W), gla_norm[l].reshape(1, GLA_W),
                        gla_ba[l].reshape(1, GLA_QK), wa2_pad[l])
        x1, h2p, rinfo, counts = _outproj(merged.reshape(n, d), x2d, mod, norm_ffn[l].reshape(1, d),
                                          w_out_b[l], w_r[l], b_r[l].reshape(1, LANES), s)
        pos, tile_expert, tile_valid = _slot_layout(rinfo, counts, n_slots // EXPERT_TILE)
        xs = _dispatch(pos, h2p, n_slots)
        ys = _experts(tile_expert, tile_valid, xs, expert_w_gate[l], expert_w_up[l], expert_w_down[l])
        x2d = _combine(pos, x1, rinfo, mod, norm_final.reshape(1, d), ys, s, final=(l == depth - 1))
    return x2d.reshape(b, s, d)
```

```python
import functools

import jax
import jax.numpy as jnp
import numpy as np
from jax import lax
from jax.experimental import pallas as pl
from jax.experimental.pallas import tpu as pltpu

F32 = jnp.float32
BF16 = jnp.bfloat16
HIGHEST = lax.Precision.HIGHEST

D_MODEL = 1024
HEAD_DIM = 64
LANES = 128
RET_W = 384
HGRN_W = 384
GLA_W = 256
GLA_QK = 128
GLA_DK = 32
GLA_RANK = 16
GLA_TAU = 16.0
CHUNK = 64
SUB = 16
ROPE_BASE = 10000.0
EPS = 1e-6
F_MIN = 1e-30
N_GROUPS = 4
EXPERTS_PER_GROUP = 8
N_EXPERTS = 32
D_EXPERT = 256
N_MOD = 6
RET_HEADS = RET_W // HEAD_DIM
N_RET_PAIRS = RET_W // LANES
N_HGRN_PAIRS = HGRN_W // LANES

C_RQ, C_RK, C_RV, C_RG = 0, 384, 768, 1152
C_HQ, C_HF, C_HI, C_HG = 1536, 1920, 2304, 2688
C_GQ, C_GK, C_GV, C_GG, C_GA = 3072, 3200, 3328, 3584, 3840
IN_PROJ_DIM = 3856
PROJ_W = C_GA + LANES

ROW_TILE = 512
MIX_TILE = 256
EXPERT_TILE = 512
PAD_BITS = EXPERT_TILE.bit_length() - 1
DISPATCH_TILE = 512
COMBINE_TILE = 256
F32_CHUNKS = D_MODEL // LANES
VMEM_LIMIT = 56 * 1024 * 1024

NT_DIMS = (((1,), (1,)), ((), ()))
TN_DIMS = (((0,), (0,)), ((), ()))


def _silu(x):
    return x * jax.nn.sigmoid(x)


def _params(*sem):
    return pltpu.CompilerParams(dimension_semantics=sem, vmem_limit_bytes=VMEM_LIMIT)


def _ada_kernel(c_ref, w_ref, b_ref, o_ref):
    ca = _silu(c_ref[...])
    o_ref[0] = jnp.dot(ca, w_ref[0], precision=HIGHEST, preferred_element_type=F32) + b_ref[0]


def _ada_mod(c, w_ada, b_ada):
    depth, d, n = w_ada.shape
    b = c.shape[0]
    tn = 1536
    return pl.pallas_call(
        _ada_kernel,
        grid=(depth, n // tn),
        in_specs=[pl.BlockSpec((b, d), lambda l, j: (0, 0)),
                  pl.BlockSpec((1, d, tn), lambda l, j: (l, 0, j)),
                  pl.BlockSpec((1, 1, tn), lambda l, j: (l, 0, j))],
        out_specs=pl.BlockSpec((1, b, tn), lambda l, j: (l, 0, j)),
        out_shape=jax.ShapeDtypeStruct((depth, b, n), F32),
        compiler_params=_params("arbitrary", "arbitrary"),
        name="ada_mod",
    )(c, w_ada, b_ada.reshape(depth, 1, n))


def _modulated_norm(x, gain, shift, scale):
    ms = jnp.mean(x * x, axis=-1, keepdims=True)
    return (x * lax.rsqrt(ms + EPS)) * gain * (1.0 + scale) + shift


def _inproj_kernel(x_ref, mod_ref, g_ref, w_ref, o_ref):
    h = _modulated_norm(x_ref[...], g_ref[...], mod_ref[0, 0:1, :], mod_ref[0, 1:2, :])
    o_ref[...] = jnp.dot(h.astype(BF16), w_ref[...], preferred_element_type=F32)


def _inproj(x2d, mod, gain, w_pad, seq):
    n, d = x2d.shape
    tm = ROW_TILE
    per_b = seq // tm
    return pl.pallas_call(
        _inproj_kernel,
        grid=(n // tm,),
        in_specs=[pl.BlockSpec((tm, d), lambda i: (i, 0)),
                  pl.BlockSpec((1, N_MOD, d), lambda i: (i // per_b, 0, 0)),
                  pl.BlockSpec((1, d), lambda i: (0, 0)),
                  pl.BlockSpec((d, PROJ_W), lambda i: (0, 0))],
        out_specs=pl.BlockSpec((tm, PROJ_W), lambda i: (i, 0)),
        out_shape=jax.ShapeDtypeStruct((n, PROJ_W), F32),
        compiler_params=_params("arbitrary"),
        name="norm_inproj",
    )(x2d, mod, gain, w_pad)


def _bf16_terms(x, n_terms):
    terms = []
    for _ in range(n_terms - 1):
        t = x.astype(BF16)
        terms.append(t)
        x = x - t.astype(F32)
    terms.append(x.astype(BF16))
    return terms


def _dot_exact_lhs(a_bf16, x, n_terms):
    return sum(jnp.dot(a_bf16, t, preferred_element_type=F32) for t in _bf16_terms(x, n_terms))


def _dot_exact_rhs(x, b_bf16, n_terms):
    return sum(jnp.dot(t, b_bf16, preferred_element_type=F32) for t in _bf16_terms(x, n_terms))


def _dot_bf16x3(a, b):
    a_hi, a_lo = _bf16_terms(a, 2)
    b_hi, b_lo = _bf16_terms(b, 2)
    return (jnp.dot(a_hi, b_hi, preferred_element_type=F32) + jnp.dot(a_lo, b_hi, preferred_element_type=F32)
            + jnp.dot(a_hi, b_lo, preferred_element_type=F32))


def _head_norm_gate(o, bdmean, gain, g):
    ms = _dot_exact_rhs(o * o, bdmean, 1)
    return o * lax.rsqrt(ms + EPS) * gain * _silu(g)


def _gated_chunks(q, k, v, cum, st, key_head_w, bd):
    t = q.shape[0]
    vw = v.shape[1]
    n_heads = LANES // key_head_w
    n_sub = CHUNK // SUB
    klane = lax.broadcasted_iota(jnp.int32, (CHUNK, LANES), 1)
    krow = lax.broadcasted_iota(jnp.int32, (CHUNK, LANES), 0)
    srow = lax.broadcasted_iota(jnp.int32, (n_heads * CHUNK, CHUNK), 0) % CHUNK
    scol = lax.broadcasted_iota(jnp.int32, (n_heads * CHUNK, CHUNK), 1)
    causal = srow >= scol
    outs = []
    for c in range(t // CHUNK):
        r = slice(c * CHUNK, (c + 1) * CHUNK)
        cm = cum[r]
        kc = k[r]
        cl = cm[CHUNK - 1:CHUNK, :]
        bounds = [jnp.zeros((1, LANES), F32)] + [cm[i * SUB - 1:i * SUB, :] for i in range(1, n_sub)]
        bmat = jnp.concatenate([jnp.broadcast_to(bi, (SUB, LANES)) for bi in bounds], axis=0)
        ebmat = jnp.concatenate([jnp.broadcast_to(jnp.exp(bi), (SUB, LANES)) for bi in bounds], axis=0)
        qt = q[r] * jnp.exp(cm - bmat)
        qe = qt * ebmat
        lhs = jnp.concatenate([jnp.where(krow // SUB == i, qt, 0.0) for i in range(n_sub)], axis=1)
        kparts = []
        for i in range(n_sub):
            hi = (i + 1) * SUB
            ki = kc[:hi] * jnp.exp(bounds[i] - cm[:hi])
            if hi < CHUNK:
                ki = jnp.concatenate([ki, jnp.zeros((CHUNK - hi, LANES), F32)], axis=0)
            kparts.append(ki)
        kstack = jnp.concatenate(kparts, axis=1).astype(BF16)
        hlane = jnp.concatenate([klane] * n_sub, axis=1) // key_head_w
        lhs_h = jnp.concatenate([jnp.where(hlane == h, lhs, 0.0) for h in range(n_heads)],
                                axis=0).astype(BF16)
        s = lax.dot_general(lhs_h, kstack, NT_DIMS, preferred_element_type=F32)
        pm = jnp.where(causal, s, 0.0).astype(BF16)
        vb = v[r].astype(BF16)
        pv = jnp.dot(pm, vb, preferred_element_type=F32)
        groups = []
        for gi in range(vw // LANES):
            h0 = gi * (LANES // HEAD_DIM)
            top = pv[h0 * CHUNK:(h0 + 1) * CHUNK, gi * LANES:(gi + 1) * LANES]
            bot = pv[(h0 + 1) * CHUNK:(h0 + 2) * CHUNK, gi * LANES:(gi + 1) * LANES]
            groups.append(jnp.where(klane < HEAD_DIM, top, bot))
        intra = groups[0] if len(groups) == 1 else jnp.concatenate(groups, axis=1)
        o = lax.dot_general(qe.astype(BF16), st.astype(BF16), NT_DIMS, preferred_element_type=F32)
        outs.append(o + intra)
        kl = (kc * jnp.exp(cl - cm)).astype(BF16)
        upd = lax.dot_general(vb, kl, TN_DIMS, preferred_element_type=F32)
        st = st * jnp.exp(cl) + jnp.where(bd, upd, 0.0)
    return jnp.concatenate(outs, axis=0), st


def _mixer_kernel(proj_ref, cos_ref, sin_ref, tri_ref, rdec_ref, xi_ref, zeta_ref, rcd_ref,
                  retn_ref, hgn_ref, lb_ref, glan_ref, ba_ref, wa2_ref,
                  o_ref, ret_st, hg_st, gl_st):
    @pl.when(pl.program_id(1) == 0)
    def _():
        ret_st[...] = jnp.zeros_like(ret_st)
        hg_st[...] = jnp.zeros_like(hg_st)
        gl_st[...] = jnp.zeros_like(gl_st)

    t = proj_ref.shape[1]
    lane = lax.broadcasted_iota(jnp.int32, (t, LANES), 1)
    sq_r = lax.broadcasted_iota(jnp.int32, (LANES, LANES), 0)
    sq_c = lax.broadcasted_iota(jnp.int32, (LANES, LANES), 1)
    bd = (sq_r // HEAD_DIM) == (sq_c // HEAD_DIM)
    bdmean = jnp.where(bd, 1.0 / HEAD_DIM, 0.0).astype(BF16)
    tri = tri_ref[...]

    def cols(c0, w=LANES):
        return proj_ref[0, :, c0:c0 + w]

    cosb = cos_ref[0]
    sinb = sin_ref[0]
    first_half = (lane % HEAD_DIM) < (HEAD_DIM // 2)

    def rope(x):
        swapped = jnp.where(first_half, pltpu.roll(x, LANES - HEAD_DIM // 2, 1),
                            pltpu.roll(x, HEAD_DIM // 2, 1))
        return x * cosb + swapped * sinb

    for p in range(N_RET_PAIRS):
        o0 = p * LANES
        q = rope(cols(C_RQ + o0))
        k = rope(cols(C_RK + o0)) * (HEAD_DIM ** -0.5)
        kb = k.astype(BF16)
        vb = cols(C_RV + o0).astype(BF16)
        st = ret_st[p]
        o = lax.dot_general((q * xi_ref[p]).astype(BF16), st.astype(BF16), NT_DIMS,
                            preferred_element_type=F32)
        intra = None
        for hh in range(2):
            qm = jnp.where(lane // HEAD_DIM == hh, q, 0.0).astype(BF16)
            s = lax.dot_general(qm, kb, NT_DIMS, preferred_element_type=F32)
            pm = (s * rdec_ref[2 * p + hh]).astype(BF16)
            oh = jnp.dot(pm, vb, preferred_element_type=F32)
            intra = oh if intra is None else jnp.where(lane < HEAD_DIM, intra, oh)
        o = o + intra
        upd = lax.dot_general(vb, (k * zeta_ref[p]).astype(BF16), TN_DIMS, preferred_element_type=F32)
        ret_st[p] = st * rcd_ref[p] + jnp.where(bd, upd, 0.0)
        y = _head_norm_gate(o, bdmean, retn_ref[:, o0:o0 + LANES], cols(C_RG + o0))
        o_ref[0, :, o0:o0 + LANES] = y.astype(o_ref.dtype)

    log_gates = []
    for p in range(N_HGRN_PAIRS):
        lb = lb_ref[:, p * LANES:(p + 1) * LANES]
        f = lb + (1.0 - lb) * jax.nn.sigmoid(cols(C_HF + p * LANES))
        log_gates.append(jnp.log(jnp.maximum(f, F_MIN)))
    u = _dot_bf16x3(cols(C_GA), wa2_ref[...]) + ba_ref[...]
    log_gates.append((jnp.minimum(u, 0.0) - jnp.log(1.0 + jnp.exp(-jnp.abs(u)))) * (1.0 / GLA_TAU))
    cum_all = _dot_exact_lhs(tri, jnp.concatenate(log_gates, axis=1), 3)

    for p in range(N_HGRN_PAIRS):
        o0 = p * LANES
        lb = lb_ref[:, o0:o0 + LANES]
        k = (1.0 - lb) * jax.nn.sigmoid(-cols(C_HF + o0))
        q = _silu(cols(C_HQ + o0)) * (HEAD_DIM ** -0.5)
        o, st = _gated_chunks(q, k, cols(C_HI + o0), cum_all[:, o0:o0 + LANES], hg_st[p], HEAD_DIM, bd)
        hg_st[p] = st
        y = _head_norm_gate(o, bdmean, hgn_ref[:, o0:o0 + LANES], cols(C_HG + o0))
        o_ref[0, :, RET_W + o0:RET_W + o0 + LANES] = y.astype(o_ref.dtype)

    cum = cum_all[:, HGRN_W:]
    ge = lax.broadcasted_iota(jnp.int32, (GLA_W, LANES), 0)
    gd = lax.broadcasted_iota(jnp.int32, (GLA_W, LANES), 1)
    bdg = (ge // HEAD_DIM) == (gd // GLA_DK)
    q = cols(C_GQ) * (GLA_DK ** -0.5)
    o, st = _gated_chunks(q, cols(C_GK), cols(C_GV, GLA_W), cum, gl_st[...], GLA_DK, bdg)
    gl_st[...] = st
    for gi in range(GLA_W // LANES):
        o0 = gi * LANES
        y = _head_norm_gate(o[:, o0:o0 + LANES], bdmean, glan_ref[:, o0:o0 + LANES], cols(C_GG + o0))
        o_ref[0, :, RET_W + HGRN_W + o0:RET_W + HGRN_W + o0 + LANES] = y.astype(o_ref.dtype)


def _mixer_constants(t):
    idx = np.arange(t)
    tri = ((idx[:, None] >= idx[None, :]) & (idx[:, None] // CHUNK == idx[None, :] // CHUNK))
    log_gamma = jnp.log1p(-jnp.exp2(-5.0 - jnp.arange(RET_HEADS, dtype=F32)))
    i = jnp.arange(t, dtype=F32)
    rel = i[:, None] - i[None, :]
    rdec = jnp.where(rel >= 0, jnp.exp(log_gamma[:, None, None] * jnp.maximum(rel, 0.0)), 0.0)
    lane_head = jnp.repeat(log_gamma, HEAD_DIM).reshape(N_RET_PAIRS, 1, LANES)
    xi = jnp.exp(lane_head * (i[None, :, None] + 1.0))
    zeta = jnp.exp(lane_head * (t - 1.0 - i[None, :, None]))
    rcd = jnp.exp(lane_head * float(t))
    return jnp.asarray(tri, BF16), rdec, xi, zeta, rcd


def _mixer(proj, cos_t, sin_t, consts, retn, hgn, lb, glan, ba, wa2p):
    b, s, _ = proj.shape
    t = MIX_TILE
    tri, rdec, xi, zeta, rcd = consts
    const2 = lambda bi, j: (0, 0)
    const3 = lambda bi, j: (0, 0, 0)
    return pl.pallas_call(
        _mixer_kernel,
        grid=(b, s // t),
        in_specs=[pl.BlockSpec((1, t, PROJ_W), lambda bi, j: (bi, j, 0)),
                  pl.BlockSpec((1, t, LANES), lambda bi, j: (bi, j, 0)),
                  pl.BlockSpec((1, t, LANES), lambda bi, j: (bi, j, 0)),
                  pl.BlockSpec((t, t), const2),
                  pl.BlockSpec((RET_HEADS, t, t), const3),
                  pl.BlockSpec((N_RET_PAIRS, t, LANES), const3),
                  pl.BlockSpec((N_RET_PAIRS, t, LANES), const3),
                  pl.BlockSpec((N_RET_PAIRS, 1, LANES), const3),
                  pl.BlockSpec((1, RET_W), const2),
                  pl.BlockSpec((1, HGRN_W), const2),
                  pl.BlockSpec((1, HGRN_W), const2),
                  pl.BlockSpec((1, GLA_W), const2),
                  pl.BlockSpec((1, GLA_QK), const2),
                  pl.BlockSpec((LANES, GLA_QK), const2)],
        out_specs=pl.BlockSpec((1, t, D_MODEL), lambda bi, j: (bi, j, 0)),
        out_shape=jax.ShapeDtypeStruct((b, s, D_MODEL), BF16),
        scratch_shapes=[pltpu.VMEM((N_RET_PAIRS, LANES, LANES), F32),
                        pltpu.VMEM((N_HGRN_PAIRS, LANES, LANES), F32),
                        pltpu.VMEM((GLA_W, LANES), F32)],
        compiler_params=_params("arbitrary", "arbitrary"),
        name="mixer",
    )(proj, cos_t, sin_t, tri, rdec, xi, zeta, rcd, retn, hgn, lb, glan, ba, wa2p)


def _route(logits):
    lane = lax.broadcasted_iota(jnp.int32, logits.shape, 1)
    neg = jnp.float32(-jnp.inf)
    big = jnp.int32(LANES)
    is_g = (lane >= N_EXPERTS) & (lane < N_EXPERTS + N_GROUPS)
    gl = jnp.where(is_g, logits, neg)
    gmax = jnp.max(gl, axis=-1, keepdims=True)
    gidx = jnp.min(jnp.where(gl == gmax, lane - N_EXPERTS, big), axis=-1, keepdims=True)
    g_w = 1.0 / jnp.sum(jnp.where(is_g, jnp.exp(gl - gmax), 0.0), axis=-1, keepdims=True)
    in_grp = (lane < N_EXPERTS) & ((lane // EXPERTS_PER_GROUP) == gidx)
    el = jnp.where(in_grp, logits, neg)
    v1 = jnp.max(el, axis=-1, keepdims=True)
    i1 = jnp.min(jnp.where(in_grp & (el == v1), lane, big), axis=-1, keepdims=True)
    rest = in_grp & (lane != i1)
    el2 = jnp.where(rest, logits, neg)
    v2 = jnp.max(el2, axis=-1, keepdims=True)
    i2 = jnp.min(jnp.where(rest & (el2 == v2), lane, big), axis=-1, keepdims=True)
    e2 = jnp.exp(v2 - v1)
    w1 = g_w / (1.0 + e2)
    w2 = g_w * e2 / (1.0 + e2)
    return i1, i2, w1, w2


R_E1, R_E2, R_RANK1, R_RANK2, R_W1, R_W2 = 0, 1, 2, 3, 4, 5


def _outproj_kernel(m_ref, x_ref, mod_ref, g_ref, wo_ref, wr_ref, br_ref, tril_ref,
                    x1_ref, h2_ref, rinfo_ref, counts_ref, carry_ref):
    @pl.when(pl.program_id(0) == 0)
    def _():
        carry_ref[...] = jnp.zeros_like(carry_ref)

    att = jnp.dot(m_ref[...], wo_ref[...], preferred_element_type=F32)
    x1 = x_ref[...] + mod_ref[0, 2:3, :] * att
    x1_ref[...] = x1
    h2 = _modulated_norm(x1, g_ref[...], mod_ref[0, 3:4, :], mod_ref[0, 4:5, :])
    _to_slabs(h2_ref, h2)
    h_hi = h2.astype(BF16)
    h_lo = (h2 - h_hi.astype(F32)).astype(BF16)
    both = jnp.dot(h_hi, wr_ref[...], preferred_element_type=F32)
    cross = jnp.dot(h_lo, wr_ref[:, :LANES], preferred_element_type=F32)
    logits = both[:, :LANES] + both[:, LANES:] + cross + br_ref[...]
    i1, i2, w1, w2 = _route(logits)
    lane = lax.broadcasted_iota(jnp.int32, logits.shape, 1)
    hit1 = lane == i1
    hit2 = lane == i2
    onehot = jnp.where(hit1 | hit2, 1.0, 0.0)
    before = carry_ref[...] + jnp.dot(tril_ref[...], onehot.astype(BF16), preferred_element_type=F32)
    r1 = jnp.sum(jnp.where(hit1, before, 0.0), axis=-1, keepdims=True)
    r2 = jnp.sum(jnp.where(hit2, before, 0.0), axis=-1, keepdims=True)
    carry_ref[...] += jnp.sum(onehot, axis=0, keepdims=True)
    counts_ref[...] = carry_ref[...]
    rec = jnp.zeros(logits.shape, F32)
    for pos, val in ((R_E1, i1.astype(F32)), (R_E2, i2.astype(F32)), (R_RANK1, r1), (R_RANK2, r2),
                     (R_W1, w1), (R_W2, w2)):
        rec = jnp.where(lane == pos, val, rec)
    rinfo_ref[...] = rec


def _outproj(merged2d, x2d, mod, gain, w_out, w_r, b_r, seq):
    n, d = x2d.shape
    tm = ROW_TILE
    per_b = seq // tm
    row = lambda i: (i, 0)
    const2 = lambda i: (0, 0)
    idx = np.arange(tm)
    tril = jnp.asarray(idx[:, None] > idx[None, :], BF16)
    return pl.pallas_call(
        _outproj_kernel,
        grid=(n // tm,),
        in_specs=[pl.BlockSpec((tm, d), row),
                  pl.BlockSpec((tm, d), row),
                  pl.BlockSpec((1, N_MOD, d), lambda i: (i // per_b, 0, 0)),
                  pl.BlockSpec((1, d), const2),
                  pl.BlockSpec((d, d), const2),
                  pl.BlockSpec((d, 2 * LANES), const2),
                  pl.BlockSpec((1, LANES), const2),
                  pl.BlockSpec((tm, tm), const2)],
        out_specs=[pl.BlockSpec((tm, d), row),
                   pl.BlockSpec((tm * F32_CHUNKS, LANES), row),
                   pl.BlockSpec((tm, LANES), row),
                   pl.BlockSpec((1, LANES), const2)],
        out_shape=[jax.ShapeDtypeStruct((n, d), F32),
                   jax.ShapeDtypeStruct((n * F32_CHUNKS, LANES), F32),
                   jax.ShapeDtypeStruct((n, LANES), F32),
                   jax.ShapeDtypeStruct((1, LANES), F32)],
        scratch_shapes=[pltpu.VMEM((1, LANES), F32)],
        compiler_params=_params("arbitrary"),
        name="outproj_route",
    )(merged2d, x2d, mod, gain, w_out, w_r, b_r, tril)


def _slot_layout(rinfo, counts, n_tiles):
    cnt = counts[0, :N_EXPERTS].astype(jnp.int32)
    padded = (cnt + EXPERT_TILE - 1) // EXPERT_TILE * EXPERT_TILE
    ends = jnp.cumsum(padded)
    starts = ends - padded
    e = rinfo[:, R_E1:R_E2 + 1].astype(jnp.int32)
    rank = rinfo[:, R_RANK1:R_RANK2 + 1].astype(jnp.int32)
    ids = jnp.arange(N_EXPERTS, dtype=jnp.int32)
    start_of = jnp.sum(jnp.where(e[:, :, None] == ids, starts, 0), axis=-1)
    pos = (start_of + rank).reshape(-1)
    tile_ids = jnp.arange(n_tiles, dtype=jnp.int32)
    tile_start = tile_ids * EXPERT_TILE
    tile_expert = jnp.sum((tile_start[:, None] >= ends[None, :]).astype(jnp.int32), axis=-1)
    tile_expert = jnp.minimum(tile_expert, N_EXPERTS - 1)
    tile_valid = (tile_start < ends[-1]).astype(jnp.int32)
    used_tiles = ends[-1:] // EXPERT_TILE
    tile_src = jnp.minimum(tile_ids, used_tiles - 1)
    return pos, starts + cnt, padded - cnt, used_tiles, tile_expert, tile_valid, tile_src


def _row_copy(src_ref, src_tok, dst_ref, dst_tok, chunks, sem):
    src = src_ref.at[pl.ds(pl.multiple_of(src_tok * chunks, chunks), chunks)]
    dst = dst_ref.at[pl.ds(pl.multiple_of(dst_tok * chunks, chunks), chunks)]
    return pltpu.make_async_copy(src, dst, sem)


def _to_slabs(ref, val):
    t = val.shape[0]
    chunks = ref.shape[0] // t
    for j in range(chunks):
        ref[pl.ds(j, t, stride=chunks), :] = val[:, j * LANES:(j + 1) * LANES]


def _from_slabs(ref, chunks):
    t = ref.shape[0] // chunks
    return jnp.concatenate([ref[pl.ds(j, t, stride=chunks), :] for j in range(chunks)], axis=1)


def _pad_copy(zbuf, xs_ref, pad_start, pad_len, bit, sem):
    rows = (1 << bit) * F32_CHUNKS
    slot = pad_start + (pad_len & ((1 << bit) - 1))
    dst = xs_ref.at[pl.ds(pl.multiple_of(slot * F32_CHUNKS, F32_CHUNKS), rows)]
    return pltpu.make_async_copy(zbuf.at[pl.ds(0, rows)], dst, sem)


def _dispatch_kernel(pos_ref, pstart_ref, plen_ref, used_ref, h_ref, xs_ref, zbuf, sem, zsem):
    @pl.when(pl.program_id(0) == 0)
    def _():
        zbuf[...] = jnp.zeros_like(zbuf)
        tile_rows = EXPERT_TILE * F32_CHUNKS

        def each(e, carry, wait):
            for bit in range(PAD_BITS):
                @pl.when(((plen_ref[e] >> bit) & 1) == 1)
                def _():
                    copy = _pad_copy(zbuf, xs_ref, pstart_ref[e], plen_ref[e], bit, zsem)
                    copy.wait() if wait else copy.start()
            return carry

        def tail(j, carry, wait):
            dst = xs_ref.at[pl.ds(pl.multiple_of(j * tile_rows, tile_rows), tile_rows)]
            copy = pltpu.make_async_copy(zbuf, dst, zsem)
            copy.wait() if wait else copy.start()
            return carry

        n_tiles = xs_ref.shape[0] // tile_rows
        for wait in (False, True):
            lax.fori_loop(0, N_EXPERTS, functools.partial(each, wait=wait), 0)
            lax.fori_loop(used_ref[0], n_tiles, functools.partial(tail, wait=wait), 0)

    td = h_ref.shape[0] // F32_CHUNKS
    base = pl.program_id(0) * td

    def issue(t, carry):
        for k in range(2):
            _row_copy(h_ref, t, xs_ref, pos_ref[2 * (base + t) + k], F32_CHUNKS, sem).start(priority=k)
        return carry

    lax.fori_loop(0, td, issue, 0, unroll=8)

    def drain(t, carry):
        for k in range(2):
            _row_copy(h_ref, 0, xs_ref, 0, F32_CHUNKS, sem).wait()
        return carry

    lax.fori_loop(0, td, drain, 0, unroll=8)


def _dispatch(pos, pad_start, pad_len, used_tiles, h2, n_slots):
    n = h2.shape[0] // F32_CHUNKS
    td = min(DISPATCH_TILE, n)
    return pl.pallas_call(
        _dispatch_kernel,
        grid_spec=pltpu.PrefetchScalarGridSpec(
            num_scalar_prefetch=4,
            grid=(n // td,),
            in_specs=[pl.BlockSpec((td * F32_CHUNKS, LANES), lambda i, *_: (i, 0))],
            out_specs=pl.BlockSpec(memory_space=pl.ANY),
            scratch_shapes=[pltpu.VMEM((EXPERT_TILE * F32_CHUNKS, LANES), F32),
                            pltpu.SemaphoreType.DMA(()),
                            pltpu.SemaphoreType.DMA(())]),
        out_shape=jax.ShapeDtypeStruct((n_slots * F32_CHUNKS, LANES), h2.dtype),
        compiler_params=_params("arbitrary"),
        name="moe_dispatch",
    )(pos, pad_start, pad_len, used_tiles, h2)


def _expert_kernel(te_ref, tv_ref, ts_ref, xs_ref, wg_ref, wu_ref, wd_ref, ys_ref, wgb, wub, wdb):
    del ts_ref
    i = pl.program_id(0)
    new_expert = jnp.logical_or(i == 0, te_ref[i] != te_ref[jnp.maximum(i - 1, 0)])

    @pl.when(new_expert)
    def _():
        wgb[...] = wg_ref[0, 0].astype(BF16)
        wub[...] = wu_ref[0, 0].astype(BF16)
        wdb[...] = wd_ref[0, 0].astype(BF16)

    @pl.when(tv_ref[i] != 0)
    def _():
        x = _from_slabs(xs_ref, F32_CHUNKS).astype(BF16)
        a = jnp.dot(x, wgb[...], preferred_element_type=F32)
        b = jnp.dot(x, wub[...], preferred_element_type=F32)
        _to_slabs(ys_ref, jnp.dot((_silu(a) * b).astype(BF16), wdb[...], preferred_element_type=F32))

    @pl.when(tv_ref[i] == 0)
    def _():
        ys_ref[...] = jnp.zeros_like(ys_ref)


def _experts(tile_expert, tile_valid, tile_src, xs, wg, wu, wd, layer):
    n_slots = xs.shape[0] // F32_CHUNKS
    d = D_MODEL
    tm = EXPERT_TILE
    by_expert = lambda i, te, tv, ts: (layer, te[i], 0, 0)
    return pl.pallas_call(
        _expert_kernel,
        grid_spec=pltpu.PrefetchScalarGridSpec(
            num_scalar_prefetch=3,
            grid=(n_slots // tm,),
            in_specs=[pl.BlockSpec((tm * F32_CHUNKS, LANES), lambda i, te, tv, ts: (ts[i], 0)),
                      pl.BlockSpec((1, 1, d, D_EXPERT), by_expert),
                      pl.BlockSpec((1, 1, d, D_EXPERT), by_expert),
                      pl.BlockSpec((1, 1, D_EXPERT, d), by_expert)],
            out_specs=pl.BlockSpec((tm * F32_CHUNKS, LANES), lambda i, te, tv, ts: (i, 0)),
            scratch_shapes=[pltpu.VMEM((d, D_EXPERT), BF16),
                            pltpu.VMEM((d, D_EXPERT), BF16),
                            pltpu.VMEM((D_EXPERT, d), BF16)]),
        out_shape=jax.ShapeDtypeStruct((n_slots * F32_CHUNKS, LANES), F32),
        compiler_params=_params("arbitrary"),
        name="moe_experts",
    )(tile_expert, tile_valid, tile_src, xs, wg, wu, wd)


def _combine_kernel(pos_ref, x1_ref, rinfo_ref, mod_ref, nf_ref, ys_ref, o_ref, buf, sem, *, final):
    i = pl.program_id(0)
    tc = x1_ref.shape[0]

    def issue(tile, slot):
        base = tile * tc

        def body(t, carry):
            for k in range(2):
                _row_copy(ys_ref, pos_ref[2 * (base + t) + k], buf.at[slot, k], t, F32_CHUNKS,
                          sem.at[slot]).start(priority=k)
            return carry

        lax.fori_loop(0, tc, body, 0, unroll=8)

    @pl.when(i == 0)
    def _():
        issue(0, 0)

    @pl.when(i + 1 < pl.num_programs(0))
    def _():
        issue(i + 1, (i + 1) % 2)

    slot = i % 2

    def drain(t, carry):
        for k in range(2):
            _row_copy(ys_ref, 0, buf.at[slot, k], 0, F32_CHUNKS, sem.at[slot]).wait()
        return carry

    lax.fori_loop(0, tc, drain, 0, unroll=8)

    rinfo = rinfo_ref[...]
    y = (rinfo[:, R_W1:R_W1 + 1] * _from_slabs(buf.at[slot, 0], F32_CHUNKS)
         + rinfo[:, R_W2:R_W2 + 1] * _from_slabs(buf.at[slot, 1], F32_CHUNKS))
    x2 = x1_ref[...] + mod_ref[0, 5:6, :] * y
    if final:
        ms = jnp.mean(x2 * x2, axis=-1, keepdims=True)
        x2 = x2 * lax.rsqrt(ms + EPS) * nf_ref[...]
    o_ref[...] = x2


def _combine(pos, x1, rinfo, mod, nf, ys, seq, final):
    n, d = x1.shape
    tc = COMBINE_TILE
    per_b = seq // tc
    row = lambda i, pos: (i, 0)
    return pl.pallas_call(
        functools.partial(_combine_kernel, final=final),
        grid_spec=pltpu.PrefetchScalarGridSpec(
            num_scalar_prefetch=1,
            grid=(n // tc,),
            in_specs=[pl.BlockSpec((tc, d), row),
                      pl.BlockSpec((tc, LANES), row),
                      pl.BlockSpec((1, N_MOD, d), lambda i, pos: (i // per_b, 0, 0)),
                      pl.BlockSpec((1, d), lambda i, pos: (0, 0)),
                      pl.BlockSpec(memory_space=pl.ANY)],
            out_specs=pl.BlockSpec((tc, d), row),
            scratch_shapes=[pltpu.VMEM((2, 2, tc * F32_CHUNKS, LANES), F32),
                            pltpu.SemaphoreType.DMA((2,))]),
        out_shape=jax.ShapeDtypeStruct((n, d), F32),
        compiler_params=_params("arbitrary"),
        name="moe_combine",
    )(pos, x1, rinfo, mod, nf, ys)


def kernel(x, c, positions, w_ada, b_ada, norm_mix, norm_ffn, w_in, ret_norm, hgrn_norm, hgrn_lb_logits,
           gla_wa2, gla_ba, gla_norm, w_out, router_group_w, router_group_b, router_expert_w,
           router_expert_b, expert_w_gate, expert_w_up, expert_w_down, norm_final):
    b, s, d = x.shape
    depth = w_ada.shape[0]
    n = b * s
    assert d == D_MODEL and s % ROW_TILE == 0 and s % MIX_TILE == 0

    inv_freq = ROPE_BASE ** (-jnp.arange(0, HEAD_DIM, 2, dtype=F32) / HEAD_DIM)
    ang = positions.astype(F32)[..., None] * inv_freq
    cos, sin = jnp.cos(ang), jnp.sin(ang)
    cos_t = jnp.concatenate([cos, cos, cos, cos], axis=-1)
    sin_t = jnp.concatenate([-sin, sin, -sin, sin], axis=-1)

    lb_w = jax.nn.softmax(hgrn_lb_logits.astype(F32), axis=0)
    lower_bounds = jnp.cumsum(lb_w, axis=0) - lb_w[0]

    consts = _mixer_constants(MIX_TILE)
    mod_all = _ada_mod(c, w_ada, b_ada).reshape(depth, b, N_MOD, d)

    w_in_pad = jnp.pad(w_in, ((0, 0), (0, 0), (0, PROJ_W - IN_PROJ_DIM))).astype(BF16)
    wa2_pad = jnp.pad(gla_wa2, ((0, 0), (0, LANES - GLA_RANK), (0, 0)))
    w_out_b = w_out.astype(BF16)
    pad_r = LANES - N_EXPERTS - N_GROUPS
    w_r = jnp.pad(jnp.concatenate([router_expert_w, router_group_w], axis=-1), ((0, 0), (0, 0), (0, pad_r)))
    w_r_hi = w_r.astype(BF16)
    w_r = jnp.concatenate([w_r_hi, (w_r - w_r_hi.astype(F32)).astype(BF16)], axis=-1)
    b_r = jnp.pad(jnp.concatenate([router_expert_b, router_group_b], axis=-1), ((0, 0), (0, pad_r)))
    n_slots = (2 * n // EXPERT_TILE + N_EXPERTS) * EXPERT_TILE

    x2d = x.reshape(n, d)
    for l in range(depth):
        mod = mod_all[l]
        proj = _inproj(x2d, mod, norm_mix[l].reshape(1, d), w_in_pad[l], s)
        merged = _mixer(proj.reshape(b, s, PROJ_W), cos_t, sin_t, consts,
                        ret_norm[l].reshape(1, RET_W), hgrn_norm[l].reshape(1, HGRN_W),
                        lower_bounds[l].reshape(1, HGRN_W), gla_norm[l].reshape(1, GLA_W),
                        gla_ba[l].reshape(1, GLA_QK), wa2_pad[l])
        x1, h2p, rinfo, counts = _outproj(merged.reshape(n, d), x2d, mod, norm_ffn[l].reshape(1, d),
                                          w_out_b[l], w_r[l], b_r[l].reshape(1, LANES), s)
        pos, pad_start, pad_len, used_tiles, tile_expert, tile_valid, tile_src = _slot_layout(
            rinfo, counts, n_slots // EXPERT_TILE)
        xs = _dispatch(pos, pad_start, pad_len, used_tiles, h2p, n_slots)
        ys = _experts(tile_expert, tile_valid, tile_src, xs, expert_w_gate, expert_w_up, expert_w_down, l)
        x2d = _combine(pos, x1, rinfo, mod, norm_final.reshape(1, d), ys, s, final=(l == depth - 1))
    return x2d.reshape(b, s, d)
```

```python
import functools

import jax
import jax.numpy as jnp
import numpy as np
from jax import lax
from jax.experimental import pallas as pl
from jax.experimental.pallas import tpu as pltpu

F32 = jnp.float32
BF16 = jnp.bfloat16
HIGHEST = lax.Precision.HIGHEST

D_MODEL = 1024
HEAD_DIM = 64
LANES = 128
SUBLANES = 8
RET_W = 384
HGRN_W = 384
GLA_W = 256
GLA_QK = 128
GLA_DK = 32
GLA_RANK = 16
GLA_TAU = 16.0
CHUNK = 64
SUB = 32
MAX_SUB_DECAY = 80.0
ROPE_BASE = 10000.0
EPS = 1e-6
F_MIN = 1e-30
N_GROUPS = 4
EXPERTS_PER_GROUP = 8
N_EXPERTS = 32
D_EXPERT = 256
N_MOD = 6
RET_HEADS = RET_W // HEAD_DIM
N_RET_PAIRS = RET_W // LANES
N_HGRN_PAIRS = HGRN_W // LANES

C_RQ, C_RK, C_RV, C_RG = 0, 384, 768, 1152
C_HQ, C_HF, C_HI, C_HG = 1536, 1920, 2304, 2688
C_GQ, C_GK, C_GV, C_GG, C_GA = 3072, 3200, 3328, 3584, 3840
IN_PROJ_DIM = 3856
PROJ_W = C_GA + LANES

ROW_TILE = 512
MIX_TILE = 256
EXPERT_TILE = 512
PAD_BITS = EXPERT_TILE.bit_length() - 1
DISPATCH_TILE = 512
COMBINE_TILE = 256
F32_CHUNKS = D_MODEL // LANES
VMEM_LIMIT = 56 * 1024 * 1024

NT_DIMS = (((1,), (1,)), ((), ()))
TN_DIMS = (((0,), (0,)), ((), ()))


def _silu(x):
    return x * jax.nn.sigmoid(x)


def _params(*sem):
    return pltpu.CompilerParams(dimension_semantics=sem, vmem_limit_bytes=VMEM_LIMIT)


def _ada_kernel(c_ref, w_ref, b_ref, o_ref):
    ca = _silu(c_ref[...])
    o_ref[0] = jnp.dot(ca, w_ref[0], precision=HIGHEST, preferred_element_type=F32) + b_ref[0]


def _ada_mod(c, w_ada, b_ada):
    depth, d, n = w_ada.shape
    b = c.shape[0]
    tn = 1536
    return pl.pallas_call(
        _ada_kernel,
        grid=(depth, n // tn),
        in_specs=[pl.BlockSpec((b, d), lambda l, j: (0, 0)),
                  pl.BlockSpec((1, d, tn), lambda l, j: (l, 0, j)),
                  pl.BlockSpec((1, 1, tn), lambda l, j: (l, 0, j))],
        out_specs=pl.BlockSpec((1, b, tn), lambda l, j: (l, 0, j)),
        out_shape=jax.ShapeDtypeStruct((depth, b, n), F32),
        compiler_params=_params("arbitrary", "arbitrary"),
        name="ada_mod",
    )(c, w_ada, b_ada.reshape(depth, 1, n))


def _modulated_norm(x, gain, shift, scale):
    ms = jnp.mean(x * x, axis=-1, keepdims=True)
    return (x * lax.rsqrt(ms + EPS)) * gain * (1.0 + scale) + shift


def _log_gates(hf, ga, lb, wa2, ba):
    f = lb + (1.0 - lb) * jax.nn.sigmoid(hf)
    u = _dot_bf16x3(ga, wa2) + ba
    log_a = (jnp.minimum(u, 0.0) - jnp.log(1.0 + jnp.exp(-jnp.abs(u)))) * (1.0 / GLA_TAU)
    return jnp.concatenate([jnp.log(jnp.maximum(f, F_MIN)), log_a], axis=1)


def _inproj_kernel(x_ref, mod_ref, g_ref, w_ref, lb_ref, wa2_ref, ba_ref, blk_ref, o_ref, dec_ref):
    h = _modulated_norm(x_ref[...], g_ref[...], mod_ref[0, 0:1, :], mod_ref[0, 1:2, :])
    proj = jnp.dot(h.astype(BF16), w_ref[...], preferred_element_type=F32)
    o_ref[...] = proj
    logs = _log_gates(proj[:, C_HF:C_HF + HGRN_W], proj[:, C_GA:C_GA + LANES], lb_ref[...], wa2_ref[...],
                      ba_ref[...])
    sums = jnp.min(_dot_exact_lhs(blk_ref[...], logs, 2), axis=1, keepdims=True)
    per_tile = MIX_TILE // SUB
    lane = lax.broadcasted_iota(jnp.int32, (1, LANES), 1)
    rec = jnp.zeros((1, LANES), F32)
    for j in range(sums.shape[0] // per_tile):
        rec = jnp.where(lane == j, jnp.min(sums[j * per_tile:(j + 1) * per_tile]), rec)
    dec_ref[0] = rec


def _inproj(x2d, mod, gain, w_pad, lb, wa2p, ba, seq):
    n, d = x2d.shape
    tm = ROW_TILE
    per_b = seq // tm
    rows = np.arange(tm)
    blk = jnp.asarray(rows[None, :] // SUB == np.arange(tm // SUB)[:, None], BF16)
    const2 = lambda i: (0, 0)
    return pl.pallas_call(
        _inproj_kernel,
        grid=(n // tm,),
        in_specs=[pl.BlockSpec((tm, d), lambda i: (i, 0)),
                  pl.BlockSpec((1, N_MOD, d), lambda i: (i // per_b, 0, 0)),
                  pl.BlockSpec((1, d), const2),
                  pl.BlockSpec((d, PROJ_W), const2),
                  pl.BlockSpec((1, HGRN_W), const2),
                  pl.BlockSpec((LANES, GLA_QK), const2),
                  pl.BlockSpec((1, GLA_QK), const2),
                  pl.BlockSpec((tm // SUB, tm), const2)],
        out_specs=[pl.BlockSpec((tm, PROJ_W), lambda i: (i, 0)),
                   pl.BlockSpec((1, 1, LANES), lambda i: (i, 0, 0))],
        out_shape=[jax.ShapeDtypeStruct((n, PROJ_W), F32),
                   jax.ShapeDtypeStruct((n // tm, 1, LANES), F32)],
        compiler_params=_params("arbitrary"),
        name="norm_inproj",
    )(x2d, mod, gain, w_pad, lb, wa2p, ba, blk)


def _bf16_terms(x, n_terms):
    terms = []
    for _ in range(n_terms - 1):
        t = x.astype(BF16)
        terms.append(t)
        x = x - t.astype(F32)
    terms.append(x.astype(BF16))
    return terms


def _dot_exact_lhs(a_bf16, x, n_terms):
    return sum(jnp.dot(a_bf16, t, preferred_element_type=F32) for t in _bf16_terms(x, n_terms))


def _dot_exact_rhs(x, b_bf16, n_terms):
    return sum(jnp.dot(t, b_bf16, preferred_element_type=F32) for t in _bf16_terms(x, n_terms))


def _dot_bf16x3(a, b):
    a_hi, a_lo = _bf16_terms(a, 2)
    b_hi, b_lo = _bf16_terms(b, 2)
    return (jnp.dot(a_hi, b_hi, preferred_element_type=F32) + jnp.dot(a_lo, b_hi, preferred_element_type=F32)
            + jnp.dot(a_hi, b_lo, preferred_element_type=F32))


def _head_norm_gate(o, bdmean, gain, g):
    ms = _dot_exact_rhs(o * o, bdmean, 1)
    return o * lax.rsqrt(ms + EPS) * gain * _silu(g)


def _sub_bounds(cm):
    return [jnp.zeros((1, cm.shape[1]), F32)] + [cm[i * SUB - 1:i * SUB, :] for i in range(1, CHUNK // SUB)]


def _intra_matmul(qc, kc, vb, cm, key_head_w):
    vw = vb.shape[1]
    n_heads = LANES // key_head_w
    n_sub = CHUNK // SUB
    klane = lax.broadcasted_iota(jnp.int32, (CHUNK, LANES), 1)
    krow = lax.broadcasted_iota(jnp.int32, (CHUNK, LANES), 0)
    srow = lax.broadcasted_iota(jnp.int32, (n_heads * CHUNK, CHUNK), 0) % CHUNK
    scol = lax.broadcasted_iota(jnp.int32, (n_heads * CHUNK, CHUNK), 1)
    causal = srow >= scol
    hlane = jnp.concatenate([klane] * n_sub, axis=1) // key_head_w
    bounds = _sub_bounds(cm)
    bmat = jnp.concatenate([jnp.broadcast_to(bi, (SUB, LANES)) for bi in bounds], axis=0)
    ebmat = jnp.concatenate([jnp.broadcast_to(jnp.exp(bi), (SUB, LANES)) for bi in bounds], axis=0)
    qt = qc * jnp.exp(cm - bmat)
    lhs = jnp.concatenate([jnp.where(krow // SUB == i, qt, 0.0) for i in range(n_sub)], axis=1)
    kparts = []
    for i in range(n_sub):
        hi = (i + 1) * SUB
        ki = kc[:hi] * jnp.exp(bounds[i] - cm[:hi])
        if hi < CHUNK:
            ki = jnp.concatenate([ki, jnp.zeros((CHUNK - hi, LANES), F32)], axis=0)
        kparts.append(ki)
    kstack = jnp.concatenate(kparts, axis=1).astype(BF16)
    lhs_h = jnp.concatenate([jnp.where(hlane == h, lhs, 0.0) for h in range(n_heads)],
                            axis=0).astype(BF16)
    s = lax.dot_general(lhs_h, kstack, NT_DIMS, preferred_element_type=F32)
    pm = jnp.where(causal, s, 0.0).astype(BF16)
    pv = jnp.dot(pm, vb, preferred_element_type=F32)
    groups = []
    for gi in range(vw // LANES):
        h0 = gi * (LANES // HEAD_DIM)
        top = pv[h0 * CHUNK:(h0 + 1) * CHUNK, gi * LANES:(gi + 1) * LANES]
        bot = pv[(h0 + 1) * CHUNK:(h0 + 2) * CHUNK, gi * LANES:(gi + 1) * LANES]
        groups.append(jnp.where(klane < HEAD_DIM, top, bot))
    intra = groups[0] if len(groups) == 1 else jnp.concatenate(groups, axis=1)
    return intra, qt * ebmat


def _intra_pairwise(q, k, v, cum, key_head_w, rows):
    q_ref, k_ref, c_ref, v_ref, o_ref = rows
    t = q.shape[0]
    vw = v.shape[1]
    q_ref[...] = q
    k_ref[...] = k
    c_ref[...] = cum
    v_ref[:, :vw] = v
    d_head = lax.broadcasted_iota(jnp.int32, (LANES, vw), 0) // key_head_w
    e_head = lax.broadcasted_iota(jnp.int32, (LANES, vw), 1) // HEAD_DIM
    head_sum = jnp.where(d_head == e_head, 1.0, 0.0).astype(BF16)
    srow = lax.broadcasted_iota(jnp.int32, (CHUNK, LANES), 0)

    def row_group(g, carry):
        i0 = pl.multiple_of(g * SUBLANES, SUBLANES)
        c0 = pl.multiple_of(i0 // CHUNK * CHUNK, CHUNK)
        q8 = q_ref[pl.ds(i0, SUBLANES), :]
        c8 = c_ref[pl.ds(i0, SUBLANES), :]
        kc = k_ref[pl.ds(c0, CHUNK), :]
        cc = c_ref[pl.ds(c0, CHUNK), :]
        vc = v_ref[pl.ds(c0, CHUNK), :vw]
        out_rows = []
        for j in range(SUBLANES):
            seen = srow + c0 <= i0 + j
            decay = jnp.exp(jnp.where(seen, c8[j:j + 1, :] - cc, 0.0))
            w = jnp.where(seen, kc * decay * q8[j:j + 1, :], 0.0)
            scores = jnp.dot(w.astype(BF16), head_sum, preferred_element_type=F32)
            out_rows.append(jnp.sum(scores * vc, axis=0, keepdims=True))
        o_ref[pl.ds(i0, SUBLANES), :vw] = jnp.concatenate(out_rows, axis=0)
        return carry

    lax.fori_loop(0, t // SUBLANES, row_group, 0)
    return o_ref[:, :vw]


def _gated_chunks(q, k, v, cum, st, key_head_w, bd, rows):
    t = q.shape[0]
    intra_rows = None if rows is None else _intra_pairwise(q, k, v, cum, key_head_w, rows)
    outs = []
    for c in range(t // CHUNK):
        r = slice(c * CHUNK, (c + 1) * CHUNK)
        cm = cum[r]
        cl = cm[CHUNK - 1:CHUNK, :]
        vb = v[r].astype(BF16)
        if rows is None:
            intra, qe = _intra_matmul(q[r], k[r], vb, cm, key_head_w)
        else:
            intra, qe = intra_rows[r], q[r] * jnp.exp(cm)
        o = lax.dot_general(qe.astype(BF16), st.astype(BF16), NT_DIMS, preferred_element_type=F32)
        outs.append(o + intra)
        kl = (k[r] * jnp.exp(cl - cm)).astype(BF16)
        upd = lax.dot_general(vb, kl, TN_DIMS, preferred_element_type=F32)
        st = st * jnp.exp(cl) + jnp.where(bd, upd, 0.0)
    return jnp.concatenate(outs, axis=0), st


def _mixer_kernel(ok_ref, proj_ref, cos_ref, sin_ref, tri_ref, rdec_ref, xi_ref, zeta_ref, rcd_ref,
                  retn_ref, hgn_ref, lb_ref, glan_ref, ba_ref, wa2_ref,
                  o_ref, ret_st, hg_st, gl_st, *rows):
    @pl.when(pl.program_id(1) == 0)
    def _():
        ret_st[...] = jnp.zeros_like(ret_st)
        hg_st[...] = jnp.zeros_like(hg_st)
        gl_st[...] = jnp.zeros_like(gl_st)

    step = functools.partial(_mixer_step, proj_ref, cos_ref, sin_ref, tri_ref, rdec_ref, xi_ref, zeta_ref,
                             rcd_ref, retn_ref, hgn_ref, lb_ref, glan_ref, ba_ref, wa2_ref, o_ref,
                             ret_st, hg_st, gl_st)
    matmul_form_ok = ok_ref[pl.program_id(0) * pl.num_programs(1) + pl.program_id(1)] != 0

    @pl.when(matmul_form_ok)
    def _():
        step(None)

    @pl.when(jnp.logical_not(matmul_form_ok))
    def _():
        step(rows)


def _mixer_step(proj_ref, cos_ref, sin_ref, tri_ref, rdec_ref, xi_ref, zeta_ref, rcd_ref, retn_ref, hgn_ref,
                lb_ref, glan_ref, ba_ref, wa2_ref, o_ref, ret_st, hg_st, gl_st, rows):
    t = proj_ref.shape[1]
    lane = lax.broadcasted_iota(jnp.int32, (t, LANES), 1)
    sq_r = lax.broadcasted_iota(jnp.int32, (LANES, LANES), 0)
    sq_c = lax.broadcasted_iota(jnp.int32, (LANES, LANES), 1)
    bd = (sq_r // HEAD_DIM) == (sq_c // HEAD_DIM)
    bdmean = jnp.where(bd, 1.0 / HEAD_DIM, 0.0).astype(BF16)

    def cols(c0, w=LANES):
        return proj_ref[0, :, c0:c0 + w]

    cosb = cos_ref[0]
    sinb = sin_ref[0]
    first_half = (lane % HEAD_DIM) < (HEAD_DIM // 2)

    def rope(x):
        swapped = jnp.where(first_half, pltpu.roll(x, LANES - HEAD_DIM // 2, 1),
                            pltpu.roll(x, HEAD_DIM // 2, 1))
        return x * cosb + swapped * sinb

    for p in range(N_RET_PAIRS):
        o0 = p * LANES
        q = rope(cols(C_RQ + o0))
        k = rope(cols(C_RK + o0)) * (HEAD_DIM ** -0.5)
        kb = k.astype(BF16)
        vb = cols(C_RV + o0).astype(BF16)
        st = ret_st[p]
        o = lax.dot_general((q * xi_ref[p]).astype(BF16), st.astype(BF16), NT_DIMS,
                            preferred_element_type=F32)
        intra = None
        for hh in range(2):
            qm = jnp.where(lane // HEAD_DIM == hh, q, 0.0).astype(BF16)
            s = lax.dot_general(qm, kb, NT_DIMS, preferred_element_type=F32)
            pm = (s * rdec_ref[2 * p + hh]).astype(BF16)
            oh = jnp.dot(pm, vb, preferred_element_type=F32)
            intra = oh if intra is None else jnp.where(lane < HEAD_DIM, intra, oh)
        o = o + intra
        upd = lax.dot_general(vb, (k * zeta_ref[p]).astype(BF16), TN_DIMS, preferred_element_type=F32)
        ret_st[p] = st * rcd_ref[p] + jnp.where(bd, upd, 0.0)
        y = _head_norm_gate(o, bdmean, retn_ref[:, o0:o0 + LANES], cols(C_RG + o0))
        o_ref[0, :, o0:o0 + LANES] = y.astype(o_ref.dtype)

    log_gates = _log_gates(cols(C_HF, HGRN_W), cols(C_GA), lb_ref[...], wa2_ref[...], ba_ref[...])
    cum_all = _dot_exact_lhs(tri_ref[...], log_gates, 3)

    for p in range(N_HGRN_PAIRS):
        o0 = p * LANES
        lb = lb_ref[:, o0:o0 + LANES]
        k = (1.0 - lb) * jax.nn.sigmoid(-cols(C_HF + o0))
        q = _silu(cols(C_HQ + o0)) * (HEAD_DIM ** -0.5)
        o, st = _gated_chunks(q, k, cols(C_HI + o0), cum_all[:, o0:o0 + LANES], hg_st[p], HEAD_DIM, bd, rows)
        hg_st[p] = st
        y = _head_norm_gate(o, bdmean, hgn_ref[:, o0:o0 + LANES], cols(C_HG + o0))
        o_ref[0, :, RET_W + o0:RET_W + o0 + LANES] = y.astype(o_ref.dtype)

    cum = cum_all[:, HGRN_W:]
    ge = lax.broadcasted_iota(jnp.int32, (GLA_W, LANES), 0)
    gd = lax.broadcasted_iota(jnp.int32, (GLA_W, LANES), 1)
    bdg = (ge // HEAD_DIM) == (gd // GLA_DK)
    q = cols(C_GQ) * (GLA_DK ** -0.5)
    o, st = _gated_chunks(q, cols(C_GK), cols(C_GV, GLA_W), cum, gl_st[...], GLA_DK, bdg, rows)
    gl_st[...] = st
    for gi in range(GLA_W // LANES):
        o0 = gi * LANES
        y = _head_norm_gate(o[:, o0:o0 + LANES], bdmean, glan_ref[:, o0:o0 + LANES], cols(C_GG + o0))
        o_ref[0, :, RET_W + HGRN_W + o0:RET_W + HGRN_W + o0 + LANES] = y.astype(o_ref.dtype)


def _mixer_constants(t):
    idx = np.arange(t)
    tri = ((idx[:, None] >= idx[None, :]) & (idx[:, None] // CHUNK == idx[None, :] // CHUNK))
    log_gamma = jnp.log1p(-jnp.exp2(-5.0 - jnp.arange(RET_HEADS, dtype=F32)))
    i = jnp.arange(t, dtype=F32)
    rel = i[:, None] - i[None, :]
    rdec = jnp.where(rel >= 0, jnp.exp(log_gamma[:, None, None] * jnp.maximum(rel, 0.0)), 0.0)
    lane_head = jnp.repeat(log_gamma, HEAD_DIM).reshape(N_RET_PAIRS, 1, LANES)
    xi = jnp.exp(lane_head * (i[None, :, None] + 1.0))
    zeta = jnp.exp(lane_head * (t - 1.0 - i[None, :, None]))
    rcd = jnp.exp(lane_head * float(t))
    return jnp.asarray(tri, BF16), rdec, xi, zeta, rcd


def _mixer(matmul_form_ok, proj, cos_t, sin_t, consts, retn, hgn, lb, glan, ba, wa2p):
    b, s, _ = proj.shape
    t = MIX_TILE
    tri, rdec, xi, zeta, rcd = consts
    tile = lambda bi, j, ok: (bi, j, 0)
    const2 = lambda bi, j, ok: (0, 0)
    const3 = lambda bi, j, ok: (0, 0, 0)
    return pl.pallas_call(
        _mixer_kernel,
        grid_spec=pltpu.PrefetchScalarGridSpec(
            num_scalar_prefetch=1,
            grid=(b, s // t),
            in_specs=[pl.BlockSpec((1, t, PROJ_W), tile),
                      pl.BlockSpec((1, t, LANES), tile),
                      pl.BlockSpec((1, t, LANES), tile),
                      pl.BlockSpec((t, t), const2),
                      pl.BlockSpec((RET_HEADS, t, t), const3),
                      pl.BlockSpec((N_RET_PAIRS, t, LANES), const3),
                      pl.BlockSpec((N_RET_PAIRS, t, LANES), const3),
                      pl.BlockSpec((N_RET_PAIRS, 1, LANES), const3),
                      pl.BlockSpec((1, RET_W), const2),
                      pl.BlockSpec((1, HGRN_W), const2),
                      pl.BlockSpec((1, HGRN_W), const2),
                      pl.BlockSpec((1, GLA_W), const2),
                      pl.BlockSpec((1, GLA_QK), const2),
                      pl.BlockSpec((LANES, GLA_QK), const2)],
            out_specs=pl.BlockSpec((1, t, D_MODEL), tile),
            scratch_shapes=[pltpu.VMEM((N_RET_PAIRS, LANES, LANES), F32),
                            pltpu.VMEM((N_HGRN_PAIRS, LANES, LANES), F32),
                            pltpu.VMEM((GLA_W, LANES), F32),
                            pltpu.VMEM((t, LANES), F32), pltpu.VMEM((t, LANES), F32),
                            pltpu.VMEM((t, LANES), F32), pltpu.VMEM((t, GLA_W), F32),
                            pltpu.VMEM((t, GLA_W), F32)]),
        out_shape=jax.ShapeDtypeStruct((b, s, D_MODEL), BF16),
        compiler_params=_params("arbitrary", "arbitrary"),
        name="mixer",
    )(matmul_form_ok, proj, cos_t, sin_t, tri, rdec, xi, zeta, rcd, retn, hgn, lb, glan, ba, wa2p)


def _route(logits):
    lane = lax.broadcasted_iota(jnp.int32, logits.shape, 1)
    neg = jnp.float32(-jnp.inf)
    big = jnp.int32(LANES)
    is_g = (lane >= N_EXPERTS) & (lane < N_EXPERTS + N_GROUPS)
    gl = jnp.where(is_g, logits, neg)
    gmax = jnp.max(gl, axis=-1, keepdims=True)
    gidx = jnp.min(jnp.where(gl == gmax, lane - N_EXPERTS, big), axis=-1, keepdims=True)
    g_w = 1.0 / jnp.sum(jnp.where(is_g, jnp.exp(gl - gmax), 0.0), axis=-1, keepdims=True)
    in_grp = (lane < N_EXPERTS) & ((lane // EXPERTS_PER_GROUP) == gidx)
    el = jnp.where(in_grp, logits, neg)
    v1 = jnp.max(el, axis=-1, keepdims=True)
    i1 = jnp.min(jnp.where(in_grp & (el == v1), lane, big), axis=-1, keepdims=True)
    rest = in_grp & (lane != i1)
    el2 = jnp.where(rest, logits, neg)
    v2 = jnp.max(el2, axis=-1, keepdims=True)
    i2 = jnp.min(jnp.where(rest & (el2 == v2), lane, big), axis=-1, keepdims=True)
    e2 = jnp.exp(v2 - v1)
    w1 = g_w / (1.0 + e2)
    w2 = g_w * e2 / (1.0 + e2)
    return i1, i2, w1, w2


R_E1, R_E2, R_RANK1, R_RANK2, R_W1, R_W2 = 0, 1, 2, 3, 4, 5


def _outproj_kernel(m_ref, x_ref, mod_ref, g_ref, wo_ref, wr_ref, br_ref, tril_ref,
                    x1_ref, h2_ref, rinfo_ref, counts_ref, carry_ref):
    @pl.when(pl.program_id(0) == 0)
    def _():
        carry_ref[...] = jnp.zeros_like(carry_ref)

    att = jnp.dot(m_ref[...], wo_ref[...], preferred_element_type=F32)
    x1 = x_ref[...] + mod_ref[0, 2:3, :] * att
    x1_ref[...] = x1
    h2 = _modulated_norm(x1, g_ref[...], mod_ref[0, 3:4, :], mod_ref[0, 4:5, :])
    _to_slabs(h2_ref, h2)
    h_hi = h2.astype(BF16)
    h_lo = (h2 - h_hi.astype(F32)).astype(BF16)
    both = jnp.dot(h_hi, wr_ref[...], preferred_element_type=F32)
    cross = jnp.dot(h_lo, wr_ref[:, :LANES], preferred_element_type=F32)
    logits = both[:, :LANES] + both[:, LANES:] + cross + br_ref[...]
    i1, i2, w1, w2 = _route(logits)
    lane = lax.broadcasted_iota(jnp.int32, logits.shape, 1)
    hit1 = lane == i1
    hit2 = lane == i2
    onehot = jnp.where(hit1 | hit2, 1.0, 0.0)
    before = carry_ref[...] + jnp.dot(tril_ref[...], onehot.astype(BF16), preferred_element_type=F32)
    r1 = jnp.sum(jnp.where(hit1, before, 0.0), axis=-1, keepdims=True)
    r2 = jnp.sum(jnp.where(hit2, before, 0.0), axis=-1, keepdims=True)
    carry_ref[...] += jnp.sum(onehot, axis=0, keepdims=True)
    counts_ref[...] = carry_ref[...]
    rec = jnp.zeros(logits.shape, F32)
    for pos, val in ((R_E1, i1.astype(F32)), (R_E2, i2.astype(F32)), (R_RANK1, r1), (R_RANK2, r2),
                     (R_W1, w1), (R_W2, w2)):
        rec = jnp.where(lane == pos, val, rec)
    rinfo_ref[...] = rec


def _outproj(merged2d, x2d, mod, gain, w_out, w_r, b_r, seq):
    n, d = x2d.shape
    tm = ROW_TILE
    per_b = seq // tm
    row = lambda i: (i, 0)
    const2 = lambda i: (0, 0)
    idx = np.arange(tm)
    tril = jnp.asarray(idx[:, None] > idx[None, :], BF16)
    return pl.pallas_call(
        _outproj_kernel,
        grid=(n // tm,),
        in_specs=[pl.BlockSpec((tm, d), row),
                  pl.BlockSpec((tm, d), row),
                  pl.BlockSpec((1, N_MOD, d), lambda i: (i // per_b, 0, 0)),
                  pl.BlockSpec((1, d), const2),
                  pl.BlockSpec((d, d), const2),
                  pl.BlockSpec((d, 2 * LANES), const2),
                  pl.BlockSpec((1, LANES), const2),
                  pl.BlockSpec((tm, tm), const2)],
        out_specs=[pl.BlockSpec((tm, d), row),
                   pl.BlockSpec((tm * F32_CHUNKS, LANES), row),
                   pl.BlockSpec((tm, LANES), row),
                   pl.BlockSpec((1, LANES), const2)],
        out_shape=[jax.ShapeDtypeStruct((n, d), F32),
                   jax.ShapeDtypeStruct((n * F32_CHUNKS, LANES), F32),
                   jax.ShapeDtypeStruct((n, LANES), F32),
                   jax.ShapeDtypeStruct((1, LANES), F32)],
        scratch_shapes=[pltpu.VMEM((1, LANES), F32)],
        compiler_params=_params("arbitrary"),
        name="outproj_route",
    )(merged2d, x2d, mod, gain, w_out, w_r, b_r, tril)


def _slot_layout(rinfo, counts, n_tiles):
    cnt = counts[0, :N_EXPERTS].astype(jnp.int32)
    padded = (cnt + EXPERT_TILE - 1) // EXPERT_TILE * EXPERT_TILE
    ends = jnp.cumsum(padded)
    starts = ends - padded
    e = rinfo[:, R_E1:R_E2 + 1].astype(jnp.int32)
    rank = rinfo[:, R_RANK1:R_RANK2 + 1].astype(jnp.int32)
    ids = jnp.arange(N_EXPERTS, dtype=jnp.int32)
    start_of = jnp.sum(jnp.where(e[:, :, None] == ids, starts, 0), axis=-1)
    pos = (start_of + rank).reshape(-1)
    tile_ids = jnp.arange(n_tiles, dtype=jnp.int32)
    tile_start = tile_ids * EXPERT_TILE
    tile_expert = jnp.sum((tile_start[:, None] >= ends[None, :]).astype(jnp.int32), axis=-1)
    tile_expert = jnp.minimum(tile_expert, N_EXPERTS - 1)
    tile_valid = (tile_start < ends[-1]).astype(jnp.int32)
    used_tiles = ends[-1:] // EXPERT_TILE
    tile_src = jnp.minimum(tile_ids, used_tiles - 1)
    return pos, starts + cnt, padded - cnt, used_tiles, tile_expert, tile_valid, tile_src


def _row_copy(src_ref, src_tok, dst_ref, dst_tok, chunks, sem):
    src = src_ref.at[pl.ds(pl.multiple_of(src_tok * chunks, chunks), chunks)]
    dst = dst_ref.at[pl.ds(pl.multiple_of(dst_tok * chunks, chunks), chunks)]
    return pltpu.make_async_copy(src, dst, sem)


def _to_slabs(ref, val):
    t = val.shape[0]
    chunks = ref.shape[0] // t
    for j in range(chunks):
        ref[pl.ds(j, t, stride=chunks), :] = val[:, j * LANES:(j + 1) * LANES]


def _from_slabs(ref, chunks):
    t = ref.shape[0] // chunks
    return jnp.concatenate([ref[pl.ds(j, t, stride=chunks), :] for j in range(chunks)], axis=1)


def _pad_copy(zbuf, xs_ref, pad_start, pad_len, bit, sem):
    rows = (1 << bit) * F32_CHUNKS
    slot = pad_start + (pad_len & ((1 << bit) - 1))
    dst = xs_ref.at[pl.ds(pl.multiple_of(slot * F32_CHUNKS, F32_CHUNKS), rows)]
    return pltpu.make_async_copy(zbuf.at[pl.ds(0, rows)], dst, sem)


def _dispatch_kernel(pos_ref, pstart_ref, plen_ref, used_ref, h_ref, xs_ref, zbuf, sem, zsem):
    @pl.when(pl.program_id(0) == 0)
    def _():
        zbuf[...] = jnp.zeros_like(zbuf)
        tile_rows = EXPERT_TILE * F32_CHUNKS

        def each(e, carry, wait):
            for bit in range(PAD_BITS):
                @pl.when(((plen_ref[e] >> bit) & 1) == 1)
                def _():
                    copy = _pad_copy(zbuf, xs_ref, pstart_ref[e], plen_ref[e], bit, zsem)
                    copy.wait() if wait else copy.start()
            return carry

        def tail(j, carry, wait):
            dst = xs_ref.at[pl.ds(pl.multiple_of(j * tile_rows, tile_rows), tile_rows)]
            copy = pltpu.make_async_copy(zbuf, dst, zsem)
            copy.wait() if wait else copy.start()
            return carry

        n_tiles = xs_ref.shape[0] // tile_rows
        for wait in (False, True):
            lax.fori_loop(0, N_EXPERTS, functools.partial(each, wait=wait), 0)
            lax.fori_loop(used_ref[0], n_tiles, functools.partial(tail, wait=wait), 0)

    td = h_ref.shape[0] // F32_CHUNKS
    base = pl.program_id(0) * td

    def issue(t, carry):
        for k in range(2):
            _row_copy(h_ref, t, xs_ref, pos_ref[2 * (base + t) + k], F32_CHUNKS, sem).start(priority=k)
        return carry

    lax.fori_loop(0, td, issue, 0, unroll=8)

    def drain(t, carry):
        for k in range(2):
            _row_copy(h_ref, 0, xs_ref, 0, F32_CHUNKS, sem).wait()
        return carry

    lax.fori_loop(0, td, drain, 0, unroll=8)


def _dispatch(pos, pad_start, pad_len, used_tiles, h2, n_slots):
    n = h2.shape[0] // F32_CHUNKS
    td = min(DISPATCH_TILE, n)
    return pl.pallas_call(
        _dispatch_kernel,
        grid_spec=pltpu.PrefetchScalarGridSpec(
            num_scalar_prefetch=4,
            grid=(n // td,),
            in_specs=[pl.BlockSpec((td * F32_CHUNKS, LANES), lambda i, *_: (i, 0))],
            out_specs=pl.BlockSpec(memory_space=pl.ANY),
            scratch_shapes=[pltpu.VMEM((EXPERT_TILE * F32_CHUNKS, LANES), F32),
                            pltpu.SemaphoreType.DMA(()),
                            pltpu.SemaphoreType.DMA(())]),
        out_shape=jax.ShapeDtypeStruct((n_slots * F32_CHUNKS, LANES), h2.dtype),
        compiler_params=_params("arbitrary"),
        name="moe_dispatch",
    )(pos, pad_start, pad_len, used_tiles, h2)


def _expert_kernel(te_ref, tv_ref, ts_ref, xs_ref, wg_ref, wu_ref, wd_ref, ys_ref, wgb, wub, wdb):
    del ts_ref
    i = pl.program_id(0)
    new_expert = jnp.logical_or(i == 0, te_ref[i] != te_ref[jnp.maximum(i - 1, 0)])

    @pl.when(new_expert)
    def _():
        wgb[...] = wg_ref[0, 0].astype(BF16)
        wub[...] = wu_ref[0, 0].astype(BF16)
        wdb[...] = wd_ref[0, 0].astype(BF16)

    @pl.when(tv_ref[i] != 0)
    def _():
        x = _from_slabs(xs_ref, F32_CHUNKS).astype(BF16)
        a = jnp.dot(x, wgb[...], preferred_element_type=F32)
        b = jnp.dot(x, wub[...], preferred_element_type=F32)
        _to_slabs(ys_ref, jnp.dot((_silu(a) * b).astype(BF16), wdb[...], preferred_element_type=F32))

    @pl.when(tv_ref[i] == 0)
    def _():
        ys_ref[...] = jnp.zeros_like(ys_ref)


def _experts(tile_expert, tile_valid, tile_src, xs, wg, wu, wd, layer):
    n_slots = xs.shape[0] // F32_CHUNKS
    d = D_MODEL
    tm = EXPERT_TILE
    by_expert = lambda i, te, tv, ts: (layer, te[i], 0, 0)
    return pl.pallas_call(
        _expert_kernel,
        grid_spec=pltpu.PrefetchScalarGridSpec(
            num_scalar_prefetch=3,
            grid=(n_slots // tm,),
            in_specs=[pl.BlockSpec((tm * F32_CHUNKS, LANES), lambda i, te, tv, ts: (ts[i], 0)),
                      pl.BlockSpec((1, 1, d, D_EXPERT), by_expert),
                      pl.BlockSpec((1, 1, d, D_EXPERT), by_expert),
                      pl.BlockSpec((1, 1, D_EXPERT, d), by_expert)],
            out_specs=pl.BlockSpec((tm * F32_CHUNKS, LANES), lambda i, te, tv, ts: (i, 0)),
            scratch_shapes=[pltpu.VMEM((d, D_EXPERT), BF16),
                            pltpu.VMEM((d, D_EXPERT), BF16),
                            pltpu.VMEM((D_EXPERT, d), BF16)]),
        out_shape=jax.ShapeDtypeStruct((n_slots * F32_CHUNKS, LANES), F32),
        compiler_params=_params("arbitrary"),
        name="moe_experts",
    )(tile_expert, tile_valid, tile_src, xs, wg, wu, wd)


def _combine_kernel(pos_ref, x1_ref, rinfo_ref, mod_ref, nf_ref, ys_ref, o_ref, buf, sem, *, final):
    i = pl.program_id(0)
    tc = x1_ref.shape[0]

    def issue(tile, slot):
        base = tile * tc

        def body(t, carry):
            for k in range(2):
                _row_copy(ys_ref, pos_ref[2 * (base + t) + k], buf.at[slot, k], t, F32_CHUNKS,
                          sem.at[slot]).start(priority=k)
            return carry

        lax.fori_loop(0, tc, body, 0, unroll=8)

    @pl.when(i == 0)
    def _():
        issue(0, 0)

    @pl.when(i + 1 < pl.num_programs(0))
    def _():
        issue(i + 1, (i + 1) % 2)

    slot = i % 2

    def drain(t, carry):
        for k in range(2):
            _row_copy(ys_ref, 0, buf.at[slot, k], 0, F32_CHUNKS, sem.at[slot]).wait()
        return carry

    lax.fori_loop(0, tc, drain, 0, unroll=8)

    rinfo = rinfo_ref[...]
    y = (rinfo[:, R_W1:R_W1 + 1] * _from_slabs(buf.at[slot, 0], F32_CHUNKS)
         + rinfo[:, R_W2:R_W2 + 1] * _from_slabs(buf.at[slot, 1], F32_CHUNKS))
    x2 = x1_ref[...] + mod_ref[0, 5:6, :] * y
    if final:
        ms = jnp.mean(x2 * x2, axis=-1, keepdims=True)
        x2 = x2 * lax.rsqrt(ms + EPS) * nf_ref[...]
    o_ref[...] = x2


def _combine(pos, x1, rinfo, mod, nf, ys, seq, final):
    n, d = x1.shape
    tc = COMBINE_TILE
    per_b = seq // tc
    row = lambda i, pos: (i, 0)
    return pl.pallas_call(
        functools.partial(_combine_kernel, final=final),
        grid_spec=pltpu.PrefetchScalarGridSpec(
            num_scalar_prefetch=1,
            grid=(n // tc,),
            in_specs=[pl.BlockSpec((tc, d), row),
                      pl.BlockSpec((tc, LANES), row),
                      pl.BlockSpec((1, N_MOD, d), lambda i, pos: (i // per_b, 0, 0)),
                      pl.BlockSpec((1, d), lambda i, pos: (0, 0)),
                      pl.BlockSpec(memory_space=pl.ANY)],
            out_specs=pl.BlockSpec((tc, d), row),
            scratch_shapes=[pltpu.VMEM((2, 2, tc * F32_CHUNKS, LANES), F32),
                            pltpu.SemaphoreType.DMA((2,))]),
        out_shape=jax.ShapeDtypeStruct((n, d), F32),
        compiler_params=_params("arbitrary"),
        name="moe_combine",
    )(pos, x1, rinfo, mod, nf, ys)


def kernel(x, c, positions, w_ada, b_ada, norm_mix, norm_ffn, w_in, ret_norm, hgrn_norm, hgrn_lb_logits,
           gla_wa2, gla_ba, gla_norm, w_out, router_group_w, router_group_b, router_expert_w,
           router_expert_b, expert_w_gate, expert_w_up, expert_w_down, norm_final):
    b, s, d = x.shape
    depth = w_ada.shape[0]
    n = b * s
    assert d == D_MODEL and s % ROW_TILE == 0 and s % MIX_TILE == 0

    inv_freq = ROPE_BASE ** (-jnp.arange(0, HEAD_DIM, 2, dtype=F32) / HEAD_DIM)
    ang = positions.astype(F32)[..., None] * inv_freq
    cos, sin = jnp.cos(ang), jnp.sin(ang)
    cos_t = jnp.concatenate([cos, cos, cos, cos], axis=-1)
    sin_t = jnp.concatenate([-sin, sin, -sin, sin], axis=-1)

    lb_w = jax.nn.softmax(hgrn_lb_logits.astype(F32), axis=0)
    lower_bounds = jnp.cumsum(lb_w, axis=0) - lb_w[0]

    consts = _mixer_constants(MIX_TILE)
    mod_all = _ada_mod(c, w_ada, b_ada).reshape(depth, b, N_MOD, d)

    w_in_pad = jnp.pad(w_in, ((0, 0), (0, 0), (0, PROJ_W - IN_PROJ_DIM))).astype(BF16)
    wa2_pad = jnp.pad(gla_wa2, ((0, 0), (0, LANES - GLA_RANK), (0, 0)))
    w_out_b = w_out.astype(BF16)
    pad_r = LANES - N_EXPERTS - N_GROUPS
    w_r = jnp.pad(jnp.concatenate([router_expert_w, router_group_w], axis=-1), ((0, 0), (0, 0), (0, pad_r)))
    w_r_hi = w_r.astype(BF16)
    w_r = jnp.concatenate([w_r_hi, (w_r - w_r_hi.astype(F32)).astype(BF16)], axis=-1)
    b_r = jnp.pad(jnp.concatenate([router_expert_b, router_group_b], axis=-1), ((0, 0), (0, pad_r)))
    n_slots = (2 * n // EXPERT_TILE + N_EXPERTS) * EXPERT_TILE

    x2d = x.reshape(n, d)
    for l in range(depth):
        mod = mod_all[l]
        lb_l = lower_bounds[l].reshape(1, HGRN_W)
        ba_l = gla_ba[l].reshape(1, GLA_QK)
        proj, sub_decay = _inproj(x2d, mod, norm_mix[l].reshape(1, d), w_in_pad[l], lb_l, wa2_pad[l], ba_l, s)
        matmul_form_ok = (sub_decay[:, 0, :ROW_TILE // MIX_TILE].reshape(-1) > -MAX_SUB_DECAY).astype(jnp.int32)
        merged = _mixer(matmul_form_ok, proj.reshape(b, s, PROJ_W), cos_t, sin_t, consts,
                        ret_norm[l].reshape(1, RET_W), hgrn_norm[l].reshape(1, HGRN_W),
                        lower_bounds[l].reshape(1, HGRN_W), gla_norm[l].reshape(1, GLA_W),
                        gla_ba[l].reshape(1, GLA_QK), wa2_pad[l])
        x1, h2p, rinfo, counts = _outproj(merged.reshape(n, d), x2d, mod, norm_ffn[l].reshape(1, d),
                                          w_out_b[l], w_r[l], b_r[l].reshape(1, LANES), s)
        pos, pad_start, pad_len, used_tiles, tile_expert, tile_valid, tile_src = _slot_layout(
            rinfo, counts, n_slots // EXPERT_TILE)
        xs = _dispatch(pos, pad_start, pad_len, used_tiles, h2p, n_slots)
        ys = _experts(tile_expert, tile_valid, tile_src, xs, expert_w_gate, expert_w_up, expert_w_down, l)
        x2d = _combine(pos, x1, rinfo, mod, norm_final.reshape(1, d), ys, s, final=(l == depth - 1))
    return x2d.reshape(b, s, d)
```

```python
import functools

import jax
import jax.numpy as jnp
import numpy as np
from jax import lax
from jax.experimental import pallas as pl
from jax.experimental.pallas import tpu as pltpu

F32 = jnp.float32
BF16 = jnp.bfloat16
HIGHEST = lax.Precision.HIGHEST

D_MODEL = 1024
HEAD_DIM = 64
LANES = 128
SUBLANES = 8
RET_W = 384
HGRN_W = 384
GLA_W = 256
GLA_QK = 128
GLA_DK = 32
GLA_RANK = 16
GLA_TAU = 16.0
CHUNK = 64
SUB = 32
MAX_SUB_DECAY = 80.0
ROPE_BASE = 10000.0
EPS = 1e-6
F_MIN = 1e-30
N_GROUPS = 4
EXPERTS_PER_GROUP = 8
N_EXPERTS = 32
D_EXPERT = 256
N_MOD = 6
RET_HEADS = RET_W // HEAD_DIM
N_RET_PAIRS = RET_W // LANES
N_HGRN_PAIRS = HGRN_W // LANES

C_RQ, C_RK, C_RV, C_RG = 0, 384, 768, 1152
C_HQ, C_HF, C_HI, C_HG = 1536, 1920, 2304, 2688
C_GQ, C_GK, C_GV, C_GG, C_GA = 3072, 3200, 3328, 3584, 3840
IN_PROJ_DIM = 3856
PROJ_W = C_GA + LANES

ROW_TILE = 512
MIX_TILE = 256
EXPERT_TILE = 512
PAD_BITS = EXPERT_TILE.bit_length() - 1
DISPATCH_TILE = 512
COMBINE_TILE = 256
F32_CHUNKS = D_MODEL // LANES
VMEM_LIMIT = 56 * 1024 * 1024

NT_DIMS = (((1,), (1,)), ((), ()))
TN_DIMS = (((0,), (0,)), ((), ()))


def _silu(x):
    return x * jax.nn.sigmoid(x)


def _params(*sem):
    return pltpu.CompilerParams(dimension_semantics=sem, vmem_limit_bytes=VMEM_LIMIT)


def _ada_kernel(c_ref, w_ref, b_ref, o_ref):
    ca = _silu(c_ref[...])
    o_ref[0] = jnp.dot(ca, w_ref[0], precision=HIGHEST, preferred_element_type=F32) + b_ref[0]


def _ada_mod(c, w_ada, b_ada):
    depth, d, n = w_ada.shape
    b = c.shape[0]
    tn = 1536
    return pl.pallas_call(
        _ada_kernel,
        grid=(depth, n // tn),
        in_specs=[pl.BlockSpec((b, d), lambda l, j: (0, 0)),
                  pl.BlockSpec((1, d, tn), lambda l, j: (l, 0, j)),
                  pl.BlockSpec((1, 1, tn), lambda l, j: (l, 0, j))],
        out_specs=pl.BlockSpec((1, b, tn), lambda l, j: (l, 0, j)),
        out_shape=jax.ShapeDtypeStruct((depth, b, n), F32),
        compiler_params=_params("arbitrary", "arbitrary"),
        name="ada_mod",
    )(c, w_ada, b_ada.reshape(depth, 1, n))


def _modulated_norm(x, gain, shift, scale):
    ms = jnp.mean(x * x, axis=-1, keepdims=True)
    return (x * lax.rsqrt(ms + EPS)) * gain * (1.0 + scale) + shift


def _log_gates(hf, ga, lb, wa2, ba):
    f = lb + (1.0 - lb) * jax.nn.sigmoid(hf)
    u = _dot_bf16x3(ga, wa2) + ba
    log_a = (jnp.minimum(u, 0.0) - jnp.log(1.0 + jnp.exp(-jnp.abs(u)))) * (1.0 / GLA_TAU)
    return jnp.concatenate([jnp.log(jnp.maximum(f, F_MIN)), log_a], axis=1)


def _inproj_kernel(x_ref, mod_ref, g_ref, w_ref, lb_ref, wa2_ref, ba_ref, blk_ref, o_ref, dec_ref):
    _project(x_ref[...], mod_ref, g_ref, w_ref, lb_ref, wa2_ref, ba_ref, blk_ref, o_ref, dec_ref)


def _project(x, mod_ref, g_ref, w_ref, lb_ref, wa2_ref, ba_ref, blk_ref, o_ref, dec_ref):
    h = _modulated_norm(x, g_ref[...], mod_ref[0, 0:1, :], mod_ref[0, 1:2, :])
    proj = jnp.dot(h.astype(BF16), w_ref[...], preferred_element_type=F32)
    o_ref[...] = proj
    logs = _log_gates(proj[:, C_HF:C_HF + HGRN_W], proj[:, C_GA:C_GA + LANES], lb_ref[...], wa2_ref[...],
                      ba_ref[...])
    sums = jnp.min(_dot_exact_lhs(blk_ref[...], logs, 2), axis=1, keepdims=True)
    per_tile = MIX_TILE // SUB
    lane = lax.broadcasted_iota(jnp.int32, (1, LANES), 1)
    rec = jnp.zeros((1, LANES), F32)
    for j in range(sums.shape[0] // per_tile):
        rec = jnp.where(lane == j, jnp.min(sums[j * per_tile:(j + 1) * per_tile]), rec)
    dec_ref[0] = rec


def _inproj(x2d, mod, gain, w_pad, lb, wa2p, ba, seq):
    n, d = x2d.shape
    tm = ROW_TILE
    per_b = seq // tm
    rows = np.arange(tm)
    blk = jnp.asarray(rows[None, :] // SUB == np.arange(tm // SUB)[:, None], BF16)
    const2 = lambda i: (0, 0)
    return pl.pallas_call(
        _inproj_kernel,
        grid=(n // tm,),
        in_specs=[pl.BlockSpec((tm, d), lambda i: (i, 0)),
                  pl.BlockSpec((1, N_MOD, d), lambda i: (i // per_b, 0, 0)),
                  pl.BlockSpec((1, d), const2),
                  pl.BlockSpec((d, PROJ_W), const2),
                  pl.BlockSpec((1, HGRN_W), const2),
                  pl.BlockSpec((LANES, GLA_QK), const2),
                  pl.BlockSpec((1, GLA_QK), const2),
                  pl.BlockSpec((tm // SUB, tm), const2)],
        out_specs=[pl.BlockSpec((tm, PROJ_W), lambda i: (i, 0)),
                   pl.BlockSpec((1, 1, LANES), lambda i: (i, 0, 0))],
        out_shape=[jax.ShapeDtypeStruct((n, PROJ_W), F32),
                   jax.ShapeDtypeStruct((n // tm, 1, LANES), F32)],
        compiler_params=_params("arbitrary"),
        name="norm_inproj",
    )(x2d, mod, gain, w_pad, lb, wa2p, ba, blk)


def _combine_inproj_kernel(pos_ref, x1_ref, rinfo_ref, modp_ref, mod_ref, g_ref, w_ref, lb_ref, wa2_ref,
                           ba_ref, blk_ref, ys_ref, x2_ref, o_ref, dec_ref, buf, sem):
    i = pl.program_id(0)
    last = pl.num_programs(0) - 1
    tm = x1_ref.shape[0]

    def copy(tile, t, k, slot):
        return _row_copy(ys_ref, pos_ref[2 * (tile * tm + t) + k], buf.at[slot, k], t, F32_CHUNKS, sem.at[slot])

    def drain(slot):
        def body(t, carry):
            for k in range(2):
                _row_copy(ys_ref, 0, buf.at[slot, k], 0, F32_CHUNKS, sem.at[slot]).wait()
            return carry
        lax.fori_loop(0, tm, body, 0, unroll=8)

    @pl.when(i == 0)
    def _():
        def body(t, carry):
            for k in range(2):
                copy(0, t, k, 0).start(priority=k)
            return carry
        lax.fori_loop(0, tm, body, 0, unroll=8)

    slot = i % 2
    drain(slot)
    nxt = jnp.minimum(i + 1, last)
    for t in range(tm):
        for k in range(2):
            copy(nxt, t, k, 1 - slot).start(priority=k)

    rinfo = rinfo_ref[...]
    y = (rinfo[:, R_W1:R_W1 + 1] * _from_slabs(buf.at[slot, 0], F32_CHUNKS)
         + rinfo[:, R_W2:R_W2 + 1] * _from_slabs(buf.at[slot, 1], F32_CHUNKS))
    x2 = x1_ref[...] + modp_ref[0, 5:6, :] * y
    x2_ref[...] = x2
    _project(x2, mod_ref, g_ref, w_ref, lb_ref, wa2_ref, ba_ref, blk_ref, o_ref, dec_ref)

    @pl.when(i == last)
    def _():
        drain(1 - slot)


def _combine_inproj(pos, x1, rinfo, mod_prev, ys, mod, gain, w_pad, lb, wa2p, ba, seq):
    n, d = x1.shape
    tm = ROW_TILE
    per_b = seq // tm
    rows = np.arange(tm)
    blk = jnp.asarray(rows[None, :] // SUB == np.arange(tm // SUB)[:, None], BF16)
    row = lambda i, pos: (i, 0)
    const2 = lambda i, pos: (0, 0)
    per_batch = lambda i, pos: (i // per_b, 0, 0)
    return pl.pallas_call(
        _combine_inproj_kernel,
        grid_spec=pltpu.PrefetchScalarGridSpec(
            num_scalar_prefetch=1,
            grid=(n // tm,),
            in_specs=[pl.BlockSpec((tm, d), row),
                      pl.BlockSpec((tm, LANES), row),
                      pl.BlockSpec((1, N_MOD, d), per_batch),
                      pl.BlockSpec((1, N_MOD, d), per_batch),
                      pl.BlockSpec((1, d), const2),
                      pl.BlockSpec((d, PROJ_W), const2, pipeline_mode=pl.Buffered(1)),
                      pl.BlockSpec((1, HGRN_W), const2),
                      pl.BlockSpec((LANES, GLA_QK), const2),
                      pl.BlockSpec((1, GLA_QK), const2),
                      pl.BlockSpec((tm // SUB, tm), const2),
                      pl.BlockSpec(memory_space=pl.ANY)],
            out_specs=[pl.BlockSpec((tm, d), row),
                       pl.BlockSpec((tm, PROJ_W), row),
                       pl.BlockSpec((1, 1, LANES), lambda i, pos: (i, 0, 0))],
            scratch_shapes=[pltpu.VMEM((2, 2, tm * F32_CHUNKS, LANES), F32),
                            pltpu.SemaphoreType.DMA((2,))]),
        out_shape=[jax.ShapeDtypeStruct((n, d), F32),
                   jax.ShapeDtypeStruct((n, PROJ_W), F32),
                   jax.ShapeDtypeStruct((n // tm, 1, LANES), F32)],
        compiler_params=_params("arbitrary"),
        name="combine_inproj",
    )(pos, x1, rinfo, mod_prev, mod, gain, w_pad, lb, wa2p, ba, blk, ys)


def _bf16_terms(x, n_terms):
    terms = []
    for _ in range(n_terms - 1):
        t = x.astype(BF16)
        terms.append(t)
        x = x - t.astype(F32)
    terms.append(x.astype(BF16))
    return terms


def _dot_exact_lhs(a_bf16, x, n_terms):
    return sum(jnp.dot(a_bf16, t, preferred_element_type=F32) for t in _bf16_terms(x, n_terms))


def _dot_exact_rhs(x, b_bf16, n_terms):
    return sum(jnp.dot(t, b_bf16, preferred_element_type=F32) for t in _bf16_terms(x, n_terms))


def _dot_bf16x3(a, b):
    a_hi, a_lo = _bf16_terms(a, 2)
    b_hi, b_lo = _bf16_terms(b, 2)
    return (jnp.dot(a_hi, b_hi, preferred_element_type=F32) + jnp.dot(a_lo, b_hi, preferred_element_type=F32)
            + jnp.dot(a_hi, b_lo, preferred_element_type=F32))


def _head_norm_gate(o, bdmean, gain, g):
    ms = _dot_exact_rhs(o * o, bdmean, 1)
    return o * lax.rsqrt(ms + EPS) * gain * _silu(g)


def _sub_bounds(cm):
    return [jnp.zeros((1, cm.shape[1]), F32)] + [cm[i * SUB - 1:i * SUB, :] for i in range(1, CHUNK // SUB)]


def _intra_matmul(qc, kc, vb, cm, key_head_w):
    vw = vb.shape[1]
    n_heads = LANES // key_head_w
    n_sub = CHUNK // SUB
    klane = lax.broadcasted_iota(jnp.int32, (CHUNK, LANES), 1)
    krow = lax.broadcasted_iota(jnp.int32, (CHUNK, LANES), 0)
    srow = lax.broadcasted_iota(jnp.int32, (n_heads * CHUNK, CHUNK), 0) % CHUNK
    scol = lax.broadcasted_iota(jnp.int32, (n_heads * CHUNK, CHUNK), 1)
    causal = srow >= scol
    hlane = jnp.concatenate([klane] * n_sub, axis=1) // key_head_w
    bounds = _sub_bounds(cm)
    bmat = jnp.concatenate([jnp.broadcast_to(bi, (SUB, LANES)) for bi in bounds], axis=0)
    ebmat = jnp.concatenate([jnp.broadcast_to(jnp.exp(bi), (SUB, LANES)) for bi in bounds], axis=0)
    qt = qc * jnp.exp(cm - bmat)
    lhs = jnp.concatenate([jnp.where(krow // SUB == i, qt, 0.0) for i in range(n_sub)], axis=1)
    kparts = []
    for i in range(n_sub):
        hi = (i + 1) * SUB
        ki = kc[:hi] * jnp.exp(bounds[i] - cm[:hi])
        if hi < CHUNK:
            ki = jnp.concatenate([ki, jnp.zeros((CHUNK - hi, LANES), F32)], axis=0)
        kparts.append(ki)
    kstack = jnp.concatenate(kparts, axis=1).astype(BF16)
    lhs_h = jnp.concatenate([jnp.where(hlane == h, lhs, 0.0) for h in range(n_heads)],
                            axis=0).astype(BF16)
    s = lax.dot_general(lhs_h, kstack, NT_DIMS, preferred_element_type=F32)
    pm = jnp.where(causal, s, 0.0).astype(BF16)
    pv = jnp.dot(pm, vb, preferred_element_type=F32)
    groups = []
    for gi in range(vw // LANES):
        h0 = gi * (LANES // HEAD_DIM)
        top = pv[h0 * CHUNK:(h0 + 1) * CHUNK, gi * LANES:(gi + 1) * LANES]
        bot = pv[(h0 + 1) * CHUNK:(h0 + 2) * CHUNK, gi * LANES:(gi + 1) * LANES]
        groups.append(jnp.where(klane < HEAD_DIM, top, bot))
    intra = groups[0] if len(groups) == 1 else jnp.concatenate(groups, axis=1)
    return intra, qt * ebmat


def _intra_pairwise(q, k, v, cum, key_head_w, rows):
    q_ref, k_ref, c_ref, v_ref, o_ref = rows
    t = q.shape[0]
    vw = v.shape[1]
    q_ref[...] = q
    k_ref[...] = k
    c_ref[...] = cum
    v_ref[:, :vw] = v
    d_head = lax.broadcasted_iota(jnp.int32, (LANES, vw), 0) // key_head_w
    e_head = lax.broadcasted_iota(jnp.int32, (LANES, vw), 1) // HEAD_DIM
    head_sum = jnp.where(d_head == e_head, 1.0, 0.0).astype(BF16)
    srow = lax.broadcasted_iota(jnp.int32, (CHUNK, LANES), 0)

    def row_group(g, carry):
        i0 = pl.multiple_of(g * SUBLANES, SUBLANES)
        c0 = pl.multiple_of(i0 // CHUNK * CHUNK, CHUNK)
        q8 = q_ref[pl.ds(i0, SUBLANES), :]
        c8 = c_ref[pl.ds(i0, SUBLANES), :]
        kc = k_ref[pl.ds(c0, CHUNK), :]
        cc = c_ref[pl.ds(c0, CHUNK), :]
        vc = v_ref[pl.ds(c0, CHUNK), :vw]
        out_rows = []
        for j in range(SUBLANES):
            seen = srow + c0 <= i0 + j
            decay = jnp.exp(jnp.where(seen, c8[j:j + 1, :] - cc, 0.0))
            w = jnp.where(seen, kc * decay * q8[j:j + 1, :], 0.0)
            scores = jnp.dot(w.astype(BF16), head_sum, preferred_element_type=F32)
            out_rows.append(jnp.sum(scores * vc, axis=0, keepdims=True))
        o_ref[pl.ds(i0, SUBLANES), :vw] = jnp.concatenate(out_rows, axis=0)
        return carry

    lax.fori_loop(0, t // SUBLANES, row_group, 0)
    return o_ref[:, :vw]


def _gated_chunks(q, k, v, cum, st, key_head_w, bd, rows):
    t = q.shape[0]
    intra_rows = None if rows is None else _intra_pairwise(q, k, v, cum, key_head_w, rows)
    outs = []
    for c in range(t // CHUNK):
        r = slice(c * CHUNK, (c + 1) * CHUNK)
        cm = cum[r]
        cl = cm[CHUNK - 1:CHUNK, :]
        vb = v[r].astype(BF16)
        if rows is None:
            intra, qe = _intra_matmul(q[r], k[r], vb, cm, key_head_w)
        else:
            intra, qe = intra_rows[r], q[r] * jnp.exp(cm)
        o = lax.dot_general(qe.astype(BF16), st.astype(BF16), NT_DIMS, preferred_element_type=F32)
        outs.append(o + intra)
        kl = (k[r] * jnp.exp(cl - cm)).astype(BF16)
        upd = lax.dot_general(vb, kl, TN_DIMS, preferred_element_type=F32)
        st = st * jnp.exp(cl) + jnp.where(bd, upd, 0.0)
    return jnp.concatenate(outs, axis=0), st


def _mixer_kernel(ok_ref, proj_ref, cos_ref, sin_ref, tri_ref, rdec_ref, xi_ref, zeta_ref, rcd_ref,
                  retn_ref, hgn_ref, lb_ref, glan_ref, ba_ref, wa2_ref,
                  o_ref, ret_st, hg_st, gl_st, *rows):
    @pl.when(pl.program_id(1) == 0)
    def _():
        ret_st[...] = jnp.zeros_like(ret_st)
        hg_st[...] = jnp.zeros_like(hg_st)
        gl_st[...] = jnp.zeros_like(gl_st)

    step = functools.partial(_mixer_step, proj_ref, cos_ref, sin_ref, tri_ref, rdec_ref, xi_ref, zeta_ref,
                             rcd_ref, retn_ref, hgn_ref, lb_ref, glan_ref, ba_ref, wa2_ref, o_ref,
                             ret_st, hg_st, gl_st)
    matmul_form_ok = ok_ref[pl.program_id(0) * pl.num_programs(1) + pl.program_id(1)] != 0

    @pl.when(matmul_form_ok)
    def _():
        step(None)

    @pl.when(jnp.logical_not(matmul_form_ok))
    def _():
        step(rows)


def _mixer_step(proj_ref, cos_ref, sin_ref, tri_ref, rdec_ref, xi_ref, zeta_ref, rcd_ref, retn_ref, hgn_ref,
                lb_ref, glan_ref, ba_ref, wa2_ref, o_ref, ret_st, hg_st, gl_st, rows):
    t = proj_ref.shape[1]
    lane = lax.broadcasted_iota(jnp.int32, (t, LANES), 1)
    sq_r = lax.broadcasted_iota(jnp.int32, (LANES, LANES), 0)
    sq_c = lax.broadcasted_iota(jnp.int32, (LANES, LANES), 1)
    bd = (sq_r // HEAD_DIM) == (sq_c // HEAD_DIM)
    bdmean = jnp.where(bd, 1.0 / HEAD_DIM, 0.0).astype(BF16)

    def cols(c0, w=LANES):
        return proj_ref[0, :, c0:c0 + w]

    cosb = cos_ref[0]
    sinb = sin_ref[0]
    first_half = (lane % HEAD_DIM) < (HEAD_DIM // 2)

    def rope(x):
        swapped = jnp.where(first_half, pltpu.roll(x, LANES - HEAD_DIM // 2, 1),
                            pltpu.roll(x, HEAD_DIM // 2, 1))
        return x * cosb + swapped * sinb

    for p in range(N_RET_PAIRS):
        o0 = p * LANES
        q = rope(cols(C_RQ + o0))
        k = rope(cols(C_RK + o0)) * (HEAD_DIM ** -0.5)
        kb = k.astype(BF16)
        vb = cols(C_RV + o0).astype(BF16)
        st = ret_st[p]
        o = lax.dot_general((q * xi_ref[p]).astype(BF16), st.astype(BF16), NT_DIMS,
                            preferred_element_type=F32)
        intra = None
        for hh in range(2):
            qm = jnp.where(lane // HEAD_DIM == hh, q, 0.0).astype(BF16)
            s = lax.dot_general(qm, kb, NT_DIMS, preferred_element_type=F32)
            pm = (s * rdec_ref[2 * p + hh]).astype(BF16)
            oh = jnp.dot(pm, vb, preferred_element_type=F32)
            intra = oh if intra is None else jnp.where(lane < HEAD_DIM, intra, oh)
        o = o + intra
        upd = lax.dot_general(vb, (k * zeta_ref[p]).astype(BF16), TN_DIMS, preferred_element_type=F32)
        ret_st[p] = st * rcd_ref[p] + jnp.where(bd, upd, 0.0)
        y = _head_norm_gate(o, bdmean, retn_ref[:, o0:o0 + LANES], cols(C_RG + o0))
        o_ref[0, :, o0:o0 + LANES] = y.astype(o_ref.dtype)

    log_gates = _log_gates(cols(C_HF, HGRN_W), cols(C_GA), lb_ref[...], wa2_ref[...], ba_ref[...])
    cum_all = _dot_exact_lhs(tri_ref[...], log_gates, 3)

    for p in range(N_HGRN_PAIRS):
        o0 = p * LANES
        lb = lb_ref[:, o0:o0 + LANES]
        k = (1.0 - lb) * jax.nn.sigmoid(-cols(C_HF + o0))
        q = _silu(cols(C_HQ + o0)) * (HEAD_DIM ** -0.5)
        o, st = _gated_chunks(q, k, cols(C_HI + o0), cum_all[:, o0:o0 + LANES], hg_st[p], HEAD_DIM, bd, rows)
        hg_st[p] = st
        y = _head_norm_gate(o, bdmean, hgn_ref[:, o0:o0 + LANES], cols(C_HG + o0))
        o_ref[0, :, RET_W + o0:RET_W + o0 + LANES] = y.astype(o_ref.dtype)

    cum = cum_all[:, HGRN_W:]
    ge = lax.broadcasted_iota(jnp.int32, (GLA_W, LANES), 0)
    gd = lax.broadcasted_iota(jnp.int32, (GLA_W, LANES), 1)
    bdg = (ge // HEAD_DIM) == (gd // GLA_DK)
    q = cols(C_GQ) * (GLA_DK ** -0.5)
    o, st = _gated_chunks(q, cols(C_GK), cols(C_GV, GLA_W), cum, gl_st[...], GLA_DK, bdg, rows)
    gl_st[...] = st
    for gi in range(GLA_W // LANES):
        o0 = gi * LANES
        y = _head_norm_gate(o[:, o0:o0 + LANES], bdmean, glan_ref[:, o0:o0 + LANES], cols(C_GG + o0))
        o_ref[0, :, RET_W + HGRN_W + o0:RET_W + HGRN_W + o0 + LANES] = y.astype(o_ref.dtype)


def _mixer_constants(t):
    idx = np.arange(t)
    tri = ((idx[:, None] >= idx[None, :]) & (idx[:, None] // CHUNK == idx[None, :] // CHUNK))
    log_gamma = jnp.log1p(-jnp.exp2(-5.0 - jnp.arange(RET_HEADS, dtype=F32)))
    i = jnp.arange(t, dtype=F32)
    rel = i[:, None] - i[None, :]
    rdec = jnp.where(rel >= 0, jnp.exp(log_gamma[:, None, None] * jnp.maximum(rel, 0.0)), 0.0)
    lane_head = jnp.repeat(log_gamma, HEAD_DIM).reshape(N_RET_PAIRS, 1, LANES)
    xi = jnp.exp(lane_head * (i[None, :, None] + 1.0))
    zeta = jnp.exp(lane_head * (t - 1.0 - i[None, :, None]))
    rcd = jnp.exp(lane_head * float(t))
    return jnp.asarray(tri, BF16), rdec, xi, zeta, rcd


def _mixer(matmul_form_ok, proj, cos_t, sin_t, consts, retn, hgn, lb, glan, ba, wa2p):
    b, s, _ = proj.shape
    t = MIX_TILE
    tri, rdec, xi, zeta, rcd = consts
    tile = lambda bi, j, ok: (bi, j, 0)
    const2 = lambda bi, j, ok: (0, 0)
    const3 = lambda bi, j, ok: (0, 0, 0)
    return pl.pallas_call(
        _mixer_kernel,
        grid_spec=pltpu.PrefetchScalarGridSpec(
            num_scalar_prefetch=1,
            grid=(b, s // t),
            in_specs=[pl.BlockSpec((1, t, PROJ_W), tile),
                      pl.BlockSpec((1, t, LANES), tile),
                      pl.BlockSpec((1, t, LANES), tile),
                      pl.BlockSpec((t, t), const2),
                      pl.BlockSpec((RET_HEADS, t, t), const3),
                      pl.BlockSpec((N_RET_PAIRS, t, LANES), const3),
                      pl.BlockSpec((N_RET_PAIRS, t, LANES), const3),
                      pl.BlockSpec((N_RET_PAIRS, 1, LANES), const3),
                      pl.BlockSpec((1, RET_W), const2),
                      pl.BlockSpec((1, HGRN_W), const2),
                      pl.BlockSpec((1, HGRN_W), const2),
                      pl.BlockSpec((1, GLA_W), const2),
                      pl.BlockSpec((1, GLA_QK), const2),
                      pl.BlockSpec((LANES, GLA_QK), const2)],
            out_specs=pl.BlockSpec((1, t, D_MODEL), tile),
            scratch_shapes=[pltpu.VMEM((N_RET_PAIRS, LANES, LANES), F32),
                            pltpu.VMEM((N_HGRN_PAIRS, LANES, LANES), F32),
                            pltpu.VMEM((GLA_W, LANES), F32),
                            pltpu.VMEM((t, LANES), F32), pltpu.VMEM((t, LANES), F32),
                            pltpu.VMEM((t, LANES), F32), pltpu.VMEM((t, GLA_W), F32),
                            pltpu.VMEM((t, GLA_W), F32)]),
        out_shape=jax.ShapeDtypeStruct((b, s, D_MODEL), BF16),
        compiler_params=_params("arbitrary", "arbitrary"),
        name="mixer",
    )(matmul_form_ok, proj, cos_t, sin_t, tri, rdec, xi, zeta, rcd, retn, hgn, lb, glan, ba, wa2p)


def _route(logits):
    lane = lax.broadcasted_iota(jnp.int32, logits.shape, 1)
    neg = jnp.float32(-jnp.inf)
    big = jnp.int32(LANES)
    is_g = (lane >= N_EXPERTS) & (lane < N_EXPERTS + N_GROUPS)
    gl = jnp.where(is_g, logits, neg)
    gmax = jnp.max(gl, axis=-1, keepdims=True)
    gidx = jnp.min(jnp.where(gl == gmax, lane - N_EXPERTS, big), axis=-1, keepdims=True)
    g_w = 1.0 / jnp.sum(jnp.where(is_g, jnp.exp(gl - gmax), 0.0), axis=-1, keepdims=True)
    in_grp = (lane < N_EXPERTS) & ((lane // EXPERTS_PER_GROUP) == gidx)
    el = jnp.where(in_grp, logits, neg)
    v1 = jnp.max(el, axis=-1, keepdims=True)
    i1 = jnp.min(jnp.where(in_grp & (el == v1), lane, big), axis=-1, keepdims=True)
    rest = in_grp & (lane != i1)
    el2 = jnp.where(rest, logits, neg)
    v2 = jnp.max(el2, axis=-1, keepdims=True)
    i2 = jnp.min(jnp.where(rest & (el2 == v2), lane, big), axis=-1, keepdims=True)
    e2 = jnp.exp(v2 - v1)
    w1 = g_w / (1.0 + e2)
    w2 = g_w * e2 / (1.0 + e2)
    return i1, i2, w1, w2


R_E1, R_E2, R_RANK1, R_RANK2, R_W1, R_W2 = 0, 1, 2, 3, 4, 5


def _outproj_kernel(m_ref, x_ref, mod_ref, g_ref, wo_ref, wr_ref, br_ref, tril_ref,
                    x1_ref, h2_ref, rinfo_ref, counts_ref, carry_ref):
    @pl.when(pl.program_id(0) == 0)
    def _():
        carry_ref[...] = jnp.zeros_like(carry_ref)

    att = jnp.dot(m_ref[...], wo_ref[...], preferred_element_type=F32)
    x1 = x_ref[...] + mod_ref[0, 2:3, :] * att
    x1_ref[...] = x1
    h2 = _modulated_norm(x1, g_ref[...], mod_ref[0, 3:4, :], mod_ref[0, 4:5, :])
    _to_slabs(h2_ref, h2)
    h_hi = h2.astype(BF16)
    h_lo = (h2 - h_hi.astype(F32)).astype(BF16)
    both = jnp.dot(h_hi, wr_ref[...], preferred_element_type=F32)
    cross = jnp.dot(h_lo, wr_ref[:, :LANES], preferred_element_type=F32)
    logits = both[:, :LANES] + both[:, LANES:] + cross + br_ref[...]
    i1, i2, w1, w2 = _route(logits)
    lane = lax.broadcasted_iota(jnp.int32, logits.shape, 1)
    hit1 = lane == i1
    hit2 = lane == i2
    onehot = jnp.where(hit1 | hit2, 1.0, 0.0)
    before = carry_ref[...] + jnp.dot(tril_ref[...], onehot.astype(BF16), preferred_element_type=F32)
    r1 = jnp.sum(jnp.where(hit1, before, 0.0), axis=-1, keepdims=True)
    r2 = jnp.sum(jnp.where(hit2, before, 0.0), axis=-1, keepdims=True)
    carry_ref[...] += jnp.sum(onehot, axis=0, keepdims=True)
    counts_ref[...] = carry_ref[...]
    rec = jnp.zeros(logits.shape, F32)
    for pos, val in ((R_E1, i1.astype(F32)), (R_E2, i2.astype(F32)), (R_RANK1, r1), (R_RANK2, r2),
                     (R_W1, w1), (R_W2, w2)):
        rec = jnp.where(lane == pos, val, rec)
    rinfo_ref[...] = rec


def _outproj(merged2d, x2d, mod, gain, w_out, w_r, b_r, seq):
    n, d = x2d.shape
    tm = ROW_TILE
    per_b = seq // tm
    row = lambda i: (i, 0)
    const2 = lambda i: (0, 0)
    idx = np.arange(tm)
    tril = jnp.asarray(idx[:, None] > idx[None, :], BF16)
    return pl.pallas_call(
        _outproj_kernel,
        grid=(n // tm,),
        in_specs=[pl.BlockSpec((tm, d), row),
                  pl.BlockSpec((tm, d), row),
                  pl.BlockSpec((1, N_MOD, d), lambda i: (i // per_b, 0, 0)),
                  pl.BlockSpec((1, d), const2),
                  pl.BlockSpec((d, d), const2),
                  pl.BlockSpec((d, 2 * LANES), const2),
                  pl.BlockSpec((1, LANES), const2),
                  pl.BlockSpec((tm, tm), const2)],
        out_specs=[pl.BlockSpec((tm, d), row),
                   pl.BlockSpec((tm * F32_CHUNKS, LANES), row),
                   pl.BlockSpec((tm, LANES), row),
                   pl.BlockSpec((1, LANES), const2)],
        out_shape=[jax.ShapeDtypeStruct((n, d), F32),
                   jax.ShapeDtypeStruct((n * F32_CHUNKS, LANES), F32),
                   jax.ShapeDtypeStruct((n, LANES), F32),
                   jax.ShapeDtypeStruct((1, LANES), F32)],
        scratch_shapes=[pltpu.VMEM((1, LANES), F32)],
        compiler_params=_params("arbitrary"),
        name="outproj_route",
    )(merged2d, x2d, mod, gain, w_out, w_r, b_r, tril)


def _slot_layout(rinfo, counts, n_tiles):
    cnt = counts[0, :N_EXPERTS].astype(jnp.int32)
    padded = (cnt + EXPERT_TILE - 1) // EXPERT_TILE * EXPERT_TILE
    ends = jnp.cumsum(padded)
    starts = ends - padded
    e = rinfo[:, R_E1:R_E2 + 1].astype(jnp.int32)
    rank = rinfo[:, R_RANK1:R_RANK2 + 1].astype(jnp.int32)
    ids = jnp.arange(N_EXPERTS, dtype=jnp.int32)
    start_of = jnp.sum(jnp.where(e[:, :, None] == ids, starts, 0), axis=-1)
    pos = (start_of + rank).reshape(-1)
    tile_ids = jnp.arange(n_tiles, dtype=jnp.int32)
    tile_start = tile_ids * EXPERT_TILE
    tile_expert = jnp.sum((tile_start[:, None] >= ends[None, :]).astype(jnp.int32), axis=-1)
    tile_expert = jnp.minimum(tile_expert, N_EXPERTS - 1)
    tile_valid = (tile_start < ends[-1]).astype(jnp.int32)
    used_tiles = ends[-1:] // EXPERT_TILE
    tile_src = jnp.minimum(tile_ids, used_tiles - 1)
    return pos, starts + cnt, padded - cnt, used_tiles, tile_expert, tile_valid, tile_src


def _row_copy(src_ref, src_tok, dst_ref, dst_tok, chunks, sem):
    src = src_ref.at[pl.ds(pl.multiple_of(src_tok * chunks, chunks), chunks)]
    dst = dst_ref.at[pl.ds(pl.multiple_of(dst_tok * chunks, chunks), chunks)]
    return pltpu.make_async_copy(src, dst, sem)


def _to_slabs(ref, val):
    t = val.shape[0]
    chunks = ref.shape[0] // t
    for j in range(chunks):
        ref[pl.ds(j, t, stride=chunks), :] = val[:, j * LANES:(j + 1) * LANES]


def _from_slabs(ref, chunks):
    t = ref.shape[0] // chunks
    return jnp.concatenate([ref[pl.ds(j, t, stride=chunks), :] for j in range(chunks)], axis=1)


def _pad_copy(zbuf, xs_ref, pad_start, pad_len, bit, sem):
    rows = (1 << bit) * F32_CHUNKS
    slot = pad_start + (pad_len & ((1 << bit) - 1))
    dst = xs_ref.at[pl.ds(pl.multiple_of(slot * F32_CHUNKS, F32_CHUNKS), rows)]
    return pltpu.make_async_copy(zbuf.at[pl.ds(0, rows)], dst, sem)


def _dispatch_kernel(pos_ref, pstart_ref, plen_ref, used_ref, h_ref, xs_ref, zbuf, sem, zsem):
    @pl.when(pl.program_id(0) == 0)
    def _():
        zbuf[...] = jnp.zeros_like(zbuf)
        tile_rows = EXPERT_TILE * F32_CHUNKS

        def each(e, carry, wait):
            for bit in range(PAD_BITS):
                @pl.when(((plen_ref[e] >> bit) & 1) == 1)
                def _():
                    copy = _pad_copy(zbuf, xs_ref, pstart_ref[e], plen_ref[e], bit, zsem)
                    copy.wait() if wait else copy.start()
            return carry

        def tail(j, carry, wait):
            dst = xs_ref.at[pl.ds(pl.multiple_of(j * tile_rows, tile_rows), tile_rows)]
            copy = pltpu.make_async_copy(zbuf, dst, zsem)
            copy.wait() if wait else copy.start()
            return carry

        n_tiles = xs_ref.shape[0] // tile_rows
        for wait in (False, True):
            lax.fori_loop(0, N_EXPERTS, functools.partial(each, wait=wait), 0)
            lax.fori_loop(used_ref[0], n_tiles, functools.partial(tail, wait=wait), 0)

    td = h_ref.shape[0] // F32_CHUNKS
    base = pl.program_id(0) * td

    def issue(t, carry):
        for k in range(2):
            _row_copy(h_ref, t, xs_ref, pos_ref[2 * (base + t) + k], F32_CHUNKS, sem).start(priority=k)
        return carry

    lax.fori_loop(0, td, issue, 0, unroll=8)

    def drain(t, carry):
        for k in range(2):
            _row_copy(h_ref, 0, xs_ref, 0, F32_CHUNKS, sem).wait()
        return carry

    lax.fori_loop(0, td, drain, 0, unroll=8)


def _dispatch(pos, pad_start, pad_len, used_tiles, h2, n_slots):
    n = h2.shape[0] // F32_CHUNKS
    td = min(DISPATCH_TILE, n)
    return pl.pallas_call(
        _dispatch_kernel,
        grid_spec=pltpu.PrefetchScalarGridSpec(
            num_scalar_prefetch=4,
            grid=(n // td,),
            in_specs=[pl.BlockSpec((td * F32_CHUNKS, LANES), lambda i, *_: (i, 0))],
            out_specs=pl.BlockSpec(memory_space=pl.ANY),
            scratch_shapes=[pltpu.VMEM((EXPERT_TILE * F32_CHUNKS, LANES), F32),
                            pltpu.SemaphoreType.DMA(()),
                            pltpu.SemaphoreType.DMA(())]),
        out_shape=jax.ShapeDtypeStruct((n_slots * F32_CHUNKS, LANES), h2.dtype),
        compiler_params=_params("arbitrary"),
        name="moe_dispatch",
    )(pos, pad_start, pad_len, used_tiles, h2)


def _expert_kernel(te_ref, tv_ref, ts_ref, xs_ref, wg_ref, wu_ref, wd_ref, ys_ref, wgb, wub, wdb):
    del ts_ref
    i = pl.program_id(0)
    new_expert = jnp.logical_or(i == 0, te_ref[i] != te_ref[jnp.maximum(i - 1, 0)])

    @pl.when(new_expert)
    def _():
        wgb[...] = wg_ref[0, 0].astype(BF16)
        wub[...] = wu_ref[0, 0].astype(BF16)
        wdb[...] = wd_ref[0, 0].astype(BF16)

    @pl.when(tv_ref[i] != 0)
    def _():
        x = _from_slabs(xs_ref, F32_CHUNKS).astype(BF16)
        a = jnp.dot(x, wgb[...], preferred_element_type=F32)
        b = jnp.dot(x, wub[...], preferred_element_type=F32)
        _to_slabs(ys_ref, jnp.dot((_silu(a) * b).astype(BF16), wdb[...], preferred_element_type=F32))

    @pl.when(tv_ref[i] == 0)
    def _():
        ys_ref[...] = jnp.zeros_like(ys_ref)


def _experts(tile_expert, tile_valid, tile_src, xs, wg, wu, wd, layer):
    n_slots = xs.shape[0] // F32_CHUNKS
    d = D_MODEL
    tm = EXPERT_TILE
    by_expert = lambda i, te, tv, ts: (layer, te[i], 0, 0)
    return pl.pallas_call(
        _expert_kernel,
        grid_spec=pltpu.PrefetchScalarGridSpec(
            num_scalar_prefetch=3,
            grid=(n_slots // tm,),
            in_specs=[pl.BlockSpec((tm * F32_CHUNKS, LANES), lambda i, te, tv, ts: (ts[i], 0)),
                      pl.BlockSpec((1, 1, d, D_EXPERT), by_expert),
                      pl.BlockSpec((1, 1, d, D_EXPERT), by_expert),
                      pl.BlockSpec((1, 1, D_EXPERT, d), by_expert)],
            out_specs=pl.BlockSpec((tm * F32_CHUNKS, LANES), lambda i, te, tv, ts: (i, 0)),
            scratch_shapes=[pltpu.VMEM((d, D_EXPERT), BF16),
                            pltpu.VMEM((d, D_EXPERT), BF16),
                            pltpu.VMEM((D_EXPERT, d), BF16)]),
        out_shape=jax.ShapeDtypeStruct((n_slots * F32_CHUNKS, LANES), F32),
        compiler_params=_params("arbitrary"),
        name="moe_experts",
    )(tile_expert, tile_valid, tile_src, xs, wg, wu, wd)


def _combine_kernel(pos_ref, x1_ref, rinfo_ref, mod_ref, nf_ref, ys_ref, o_ref, buf, sem, *, final):
    i = pl.program_id(0)
    tc = x1_ref.shape[0]

    def issue(tile, slot):
        base = tile * tc

        def body(t, carry):
            for k in range(2):
                _row_copy(ys_ref, pos_ref[2 * (base + t) + k], buf.at[slot, k], t, F32_CHUNKS,
                          sem.at[slot]).start(priority=k)
            return carry

        lax.fori_loop(0, tc, body, 0, unroll=8)

    @pl.when(i == 0)
    def _():
        issue(0, 0)

    @pl.when(i + 1 < pl.num_programs(0))
    def _():
        issue(i + 1, (i + 1) % 2)

    slot = i % 2

    def drain(t, carry):
        for k in range(2):
            _row_copy(ys_ref, 0, buf.at[slot, k], 0, F32_CHUNKS, sem.at[slot]).wait()
        return carry

    lax.fori_loop(0, tc, drain, 0, unroll=8)

    rinfo = rinfo_ref[...]
    y = (rinfo[:, R_W1:R_W1 + 1] * _from_slabs(buf.at[slot, 0], F32_CHUNKS)
         + rinfo[:, R_W2:R_W2 + 1] * _from_slabs(buf.at[slot, 1], F32_CHUNKS))
    x2 = x1_ref[...] + mod_ref[0, 5:6, :] * y
    if final:
        ms = jnp.mean(x2 * x2, axis=-1, keepdims=True)
        x2 = x2 * lax.rsqrt(ms + EPS) * nf_ref[...]
    o_ref[...] = x2


def _combine(pos, x1, rinfo, mod, nf, ys, seq, final):
    n, d = x1.shape
    tc = COMBINE_TILE
    per_b = seq // tc
    row = lambda i, pos: (i, 0)
    return pl.pallas_call(
        functools.partial(_combine_kernel, final=final),
        grid_spec=pltpu.PrefetchScalarGridSpec(
            num_scalar_prefetch=1,
            grid=(n // tc,),
            in_specs=[pl.BlockSpec((tc, d), row),
                      pl.BlockSpec((tc, LANES), row),
                      pl.BlockSpec((1, N_MOD, d), lambda i, pos: (i // per_b, 0, 0)),
                      pl.BlockSpec((1, d), lambda i, pos: (0, 0)),
                      pl.BlockSpec(memory_space=pl.ANY)],
            out_specs=pl.BlockSpec((tc, d), row),
            scratch_shapes=[pltpu.VMEM((2, 2, tc * F32_CHUNKS, LANES), F32),
                            pltpu.SemaphoreType.DMA((2,))]),
        out_shape=jax.ShapeDtypeStruct((n, d), F32),
        compiler_params=_params("arbitrary"),
        name="moe_combine",
    )(pos, x1, rinfo, mod, nf, ys)


def kernel(x, c, positions, w_ada, b_ada, norm_mix, norm_ffn, w_in, ret_norm, hgrn_norm, hgrn_lb_logits,
           gla_wa2, gla_ba, gla_norm, w_out, router_group_w, router_group_b, router_expert_w,
           router_expert_b, expert_w_gate, expert_w_up, expert_w_down, norm_final):
    b, s, d = x.shape
    depth = w_ada.shape[0]
    n = b * s
    assert d == D_MODEL and s % ROW_TILE == 0 and s % MIX_TILE == 0

    inv_freq = ROPE_BASE ** (-jnp.arange(0, HEAD_DIM, 2, dtype=F32) / HEAD_DIM)
    ang = positions.astype(F32)[..., None] * inv_freq
    cos, sin = jnp.cos(ang), jnp.sin(ang)
    cos_t = jnp.concatenate([cos, cos, cos, cos], axis=-1)
    sin_t = jnp.concatenate([-sin, sin, -sin, sin], axis=-1)

    lb_w = jax.nn.softmax(hgrn_lb_logits.astype(F32), axis=0)
    lower_bounds = jnp.cumsum(lb_w, axis=0) - lb_w[0]

    consts = _mixer_constants(MIX_TILE)
    mod_all = _ada_mod(c, w_ada, b_ada).reshape(depth, b, N_MOD, d)

    w_in_pad = jnp.pad(w_in, ((0, 0), (0, 0), (0, PROJ_W - IN_PROJ_DIM))).astype(BF16)
    wa2_pad = jnp.pad(gla_wa2, ((0, 0), (0, LANES - GLA_RANK), (0, 0)))
    w_out_b = w_out.astype(BF16)
    pad_r = LANES - N_EXPERTS - N_GROUPS
    w_r = jnp.pad(jnp.concatenate([router_expert_w, router_group_w], axis=-1), ((0, 0), (0, 0), (0, pad_r)))
    w_r_hi = w_r.astype(BF16)
    w_r = jnp.concatenate([w_r_hi, (w_r - w_r_hi.astype(F32)).astype(BF16)], axis=-1)
    b_r = jnp.pad(jnp.concatenate([router_expert_b, router_group_b], axis=-1), ((0, 0), (0, pad_r)))
    n_slots = (2 * n // EXPERT_TILE + N_EXPERTS) * EXPERT_TILE

    x2d = x.reshape(n, d)
    pending = None
    for l in range(depth):
        mod = mod_all[l]
        lb_l = lower_bounds[l].reshape(1, HGRN_W)
        ba_l = gla_ba[l].reshape(1, GLA_QK)
        if pending is None:
            proj, sub_decay = _inproj(x2d, mod, norm_mix[l].reshape(1, d), w_in_pad[l], lb_l, wa2_pad[l], ba_l, s)
        else:
            x2d, proj, sub_decay = _combine_inproj(*pending, mod, norm_mix[l].reshape(1, d), w_in_pad[l], lb_l,
                                                   wa2_pad[l], ba_l, s)
        matmul_form_ok = (sub_decay[:, 0, :ROW_TILE // MIX_TILE].reshape(-1) > -MAX_SUB_DECAY).astype(jnp.int32)
        merged = _mixer(matmul_form_ok, proj.reshape(b, s, PROJ_W), cos_t, sin_t, consts,
                        ret_norm[l].reshape(1, RET_W), hgrn_norm[l].reshape(1, HGRN_W),
                        lower_bounds[l].reshape(1, HGRN_W), gla_norm[l].reshape(1, GLA_W),
                        gla_ba[l].reshape(1, GLA_QK), wa2_pad[l])
        x1, h2p, rinfo, counts = _outproj(merged.reshape(n, d), x2d, mod, norm_ffn[l].reshape(1, d),
                                          w_out_b[l], w_r[l], b_r[l].reshape(1, LANES), s)
        pos, pad_start, pad_len, used_tiles, tile_expert, tile_valid, tile_src = _slot_layout(
            rinfo, counts, n_slots // EXPERT_TILE)
        xs = _dispatch(pos, pad_start, pad_len, used_tiles, h2p, n_slots)
        ys = _experts(tile_expert, tile_valid, tile_src, xs, expert_w_gate, expert_w_up, expert_w_down, l)
        pending = (pos, x1, rinfo, mod, ys)
    pos, x1, rinfo, mod, ys = pending
    return _combine(pos, x1, rinfo, mod, norm_final.reshape(1, d), ys, s, final=True).reshape(b, s, d)
```

```python
import functools

import jax
import jax.numpy as jnp
import numpy as np
from jax import lax
from jax.experimental import pallas as pl
from jax.experimental.pallas import tpu as pltpu

F32 = jnp.float32
BF16 = jnp.bfloat16
HIGHEST = lax.Precision.HIGHEST

D_MODEL = 1024
HEAD_DIM = 64
LANES = 128
SUBLANES = 8
RET_W = 384
HGRN_W = 384
GLA_W = 256
GLA_QK = 128
GLA_DK = 32
GLA_RANK = 16
GLA_TAU = 16.0
CHUNK = 64
SUB = 32
MAX_SUB_DECAY = 80.0
ROPE_BASE = 10000.0
EPS = 1e-6
F_MIN = 1e-30
N_GROUPS = 4
EXPERTS_PER_GROUP = 8
N_EXPERTS = 32
D_EXPERT = 256
N_MOD = 6
RET_HEADS = RET_W // HEAD_DIM
N_RET_PAIRS = RET_W // LANES
N_HGRN_PAIRS = HGRN_W // LANES

C_RQ, C_RK, C_RV, C_RG = 0, 384, 768, 1152
C_HQ, C_HF, C_HI, C_HG = 1536, 1920, 2304, 2688
C_GQ, C_GK, C_GV, C_GG, C_GA = 3072, 3200, 3328, 3584, 3840
IN_PROJ_DIM = 3856
PROJ_W = C_GA + LANES

ROW_TILE = 512
MIX_TILE = 256
EXPERT_TILE = 512
PAD_BITS = EXPERT_TILE.bit_length() - 1
DISPATCH_TILE = 512
COMBINE_TILE = 256
F32_CHUNKS = D_MODEL // LANES
VMEM_LIMIT = 56 * 1024 * 1024

NT_DIMS = (((1,), (1,)), ((), ()))
TN_DIMS = (((0,), (0,)), ((), ()))


def _silu(x):
    return x * jax.nn.sigmoid(x)


def _params(*sem):
    return pltpu.CompilerParams(dimension_semantics=sem, vmem_limit_bytes=VMEM_LIMIT)


def _ada_kernel(c_ref, w_ref, b_ref, o_ref):
    ca = _silu(c_ref[...])
    o_ref[0] = jnp.dot(ca, w_ref[0], precision=HIGHEST, preferred_element_type=F32) + b_ref[0]


def _ada_mod(c, w_ada, b_ada):
    depth, d, n = w_ada.shape
    b = c.shape[0]
    tn = 1536
    return pl.pallas_call(
        _ada_kernel,
        grid=(depth, n // tn),
        in_specs=[pl.BlockSpec((b, d), lambda l, j: (0, 0)),
                  pl.BlockSpec((1, d, tn), lambda l, j: (l, 0, j)),
                  pl.BlockSpec((1, 1, tn), lambda l, j: (l, 0, j))],
        out_specs=pl.BlockSpec((1, b, tn), lambda l, j: (l, 0, j)),
        out_shape=jax.ShapeDtypeStruct((depth, b, n), F32),
        compiler_params=_params("arbitrary", "arbitrary"),
        name="ada_mod",
    )(c, w_ada, b_ada.reshape(depth, 1, n))


def _modulated_norm(x, gain, shift, scale):
    ms = jnp.mean(x * x, axis=-1, keepdims=True)
    return (x * lax.rsqrt(ms + EPS)) * gain * (1.0 + scale) + shift


def _log_gates(hf, ga, lb, wa2, ba):
    f = lb + (1.0 - lb) * jax.nn.sigmoid(hf)
    u = _dot_bf16x3(ga, wa2) + ba
    log_a = (jnp.minimum(u, 0.0) - jnp.log(1.0 + jnp.exp(-jnp.abs(u)))) * (1.0 / GLA_TAU)
    return jnp.concatenate([jnp.log(jnp.maximum(f, F_MIN)), log_a], axis=1)


def _inproj_kernel(x_ref, mod_ref, g_ref, w_ref, lb_ref, wa2_ref, ba_ref, blk_ref, o_ref, dec_ref):
    h = _modulated_norm(x_ref[...], g_ref[...], mod_ref[0, 0:1, :], mod_ref[0, 1:2, :])
    proj = jnp.dot(h.astype(BF16), w_ref[...], preferred_element_type=F32)
    o_ref[...] = proj
    logs = _log_gates(proj[:, C_HF:C_HF + HGRN_W], proj[:, C_GA:C_GA + LANES], lb_ref[...], wa2_ref[...],
                      ba_ref[...])
    sums = jnp.min(_dot_exact_lhs(blk_ref[...], logs, 2), axis=1, keepdims=True)
    per_tile = MIX_TILE // SUB
    lane = lax.broadcasted_iota(jnp.int32, (1, LANES), 1)
    rec = jnp.zeros((1, LANES), F32)
    for j in range(sums.shape[0] // per_tile):
        rec = jnp.where(lane == j, jnp.min(sums[j * per_tile:(j + 1) * per_tile]), rec)
    dec_ref[0] = rec


def _inproj(x2d, mod, gain, w_pad, lb, wa2p, ba, seq):
    n, d = x2d.shape
    tm = ROW_TILE
    per_b = seq // tm
    rows = np.arange(tm)
    blk = jnp.asarray(rows[None, :] // SUB == np.arange(tm // SUB)[:, None], BF16)
    const2 = lambda i: (0, 0)
    return pl.pallas_call(
        _inproj_kernel,
        grid=(n // tm,),
        in_specs=[pl.BlockSpec((tm, d), lambda i: (i, 0)),
                  pl.BlockSpec((1, N_MOD, d), lambda i: (i // per_b, 0, 0)),
                  pl.BlockSpec((1, d), const2),
                  pl.BlockSpec((d, PROJ_W), const2),
                  pl.BlockSpec((1, HGRN_W), const2),
                  pl.BlockSpec((LANES, GLA_QK), const2),
                  pl.BlockSpec((1, GLA_QK), const2),
                  pl.BlockSpec((tm // SUB, tm), const2)],
        out_specs=[pl.BlockSpec((tm, PROJ_W), lambda i: (i, 0)),
                   pl.BlockSpec((1, 1, LANES), lambda i: (i, 0, 0))],
        out_shape=[jax.ShapeDtypeStruct((n, PROJ_W), F32),
                   jax.ShapeDtypeStruct((n // tm, 1, LANES), F32)],
        compiler_params=_params("arbitrary"),
        name="norm_inproj",
    )(x2d, mod, gain, w_pad, lb, wa2p, ba, blk)


def _bf16_terms(x, n_terms):
    terms = []
    for _ in range(n_terms - 1):
        t = x.astype(BF16)
        terms.append(t)
        x = x - t.astype(F32)
    terms.append(x.astype(BF16))
    return terms


def _dot_exact_lhs(a_bf16, x, n_terms):
    return sum(jnp.dot(a_bf16, t, preferred_element_type=F32) for t in _bf16_terms(x, n_terms))


def _dot_exact_rhs(x, b_bf16, n_terms):
    return sum(jnp.dot(t, b_bf16, preferred_element_type=F32) for t in _bf16_terms(x, n_terms))


def _dot_bf16x3(a, b):
    a_hi, a_lo = _bf16_terms(a, 2)
    b_hi, b_lo = _bf16_terms(b, 2)
    return (jnp.dot(a_hi, b_hi, preferred_element_type=F32) + jnp.dot(a_lo, b_hi, preferred_element_type=F32)
            + jnp.dot(a_hi, b_lo, preferred_element_type=F32))


def _head_norm_gate(o, bdmean, gain, g):
    ms = _dot_exact_rhs(o * o, bdmean, 1)
    return o * lax.rsqrt(ms + EPS) * gain * _silu(g)


def _sub_bounds(cm):
    return [jnp.zeros((1, cm.shape[1]), F32)] + [cm[i * SUB - 1:i * SUB, :] for i in range(1, CHUNK // SUB)]


def _intra_matmul(qc, kc, vb, cm, key_head_w):
    vw = vb.shape[1]
    n_heads = LANES // key_head_w
    n_sub = CHUNK // SUB
    klane = lax.broadcasted_iota(jnp.int32, (CHUNK, LANES), 1)
    krow = lax.broadcasted_iota(jnp.int32, (CHUNK, LANES), 0)
    srow = lax.broadcasted_iota(jnp.int32, (n_heads * CHUNK, CHUNK), 0) % CHUNK
    scol = lax.broadcasted_iota(jnp.int32, (n_heads * CHUNK, CHUNK), 1)
    causal = srow >= scol
    hlane = jnp.concatenate([klane] * n_sub, axis=1) // key_head_w
    bounds = _sub_bounds(cm)
    bmat = jnp.concatenate([jnp.broadcast_to(bi, (SUB, LANES)) for bi in bounds], axis=0)
    ebmat = jnp.concatenate([jnp.broadcast_to(jnp.exp(bi), (SUB, LANES)) for bi in bounds], axis=0)
    qt = qc * jnp.exp(cm - bmat)
    lhs = jnp.concatenate([jnp.where(krow // SUB == i, qt, 0.0) for i in range(n_sub)], axis=1)
    kparts = []
    for i in range(n_sub):
        hi = (i + 1) * SUB
        ki = kc[:hi] * jnp.exp(bounds[i] - cm[:hi])
        if hi < CHUNK:
            ki = jnp.concatenate([ki, jnp.zeros((CHUNK - hi, LANES), F32)], axis=0)
        kparts.append(ki)
    kstack = jnp.concatenate(kparts, axis=1).astype(BF16)
    lhs_h = jnp.concatenate([jnp.where(hlane == h, lhs, 0.0) for h in range(n_heads)],
                            axis=0).astype(BF16)
    s = lax.dot_general(lhs_h, kstack, NT_DIMS, preferred_element_type=F32)
    pm = jnp.where(causal, s, 0.0).astype(BF16)
    pv = jnp.dot(pm, vb, preferred_element_type=F32)
    groups = []
    for gi in range(vw // LANES):
        h0 = gi * (LANES // HEAD_DIM)
        top = pv[h0 * CHUNK:(h0 + 1) * CHUNK, gi * LANES:(gi + 1) * LANES]
        bot = pv[(h0 + 1) * CHUNK:(h0 + 2) * CHUNK, gi * LANES:(gi + 1) * LANES]
        groups.append(jnp.where(klane < HEAD_DIM, top, bot))
    intra = groups[0] if len(groups) == 1 else jnp.concatenate(groups, axis=1)
    return intra, qt * ebmat


def _intra_pairwise(q, k, v, cum, key_head_w, rows):
    q_ref, k_ref, c_ref, v_ref, o_ref = rows
    t = q.shape[0]
    vw = v.shape[1]
    q_ref[...] = q
    k_ref[...] = k
    c_ref[...] = cum
    v_ref[:, :vw] = v
    d_head = lax.broadcasted_iota(jnp.int32, (LANES, vw), 0) // key_head_w
    e_head = lax.broadcasted_iota(jnp.int32, (LANES, vw), 1) // HEAD_DIM
    head_sum = jnp.where(d_head == e_head, 1.0, 0.0).astype(BF16)
    srow = lax.broadcasted_iota(jnp.int32, (CHUNK, LANES), 0)

    def row_group(g, carry):
        i0 = pl.multiple_of(g * SUBLANES, SUBLANES)
        c0 = pl.multiple_of(i0 // CHUNK * CHUNK, CHUNK)
        q8 = q_ref[pl.ds(i0, SUBLANES), :]
        c8 = c_ref[pl.ds(i0, SUBLANES), :]
        kc = k_ref[pl.ds(c0, CHUNK), :]
        cc = c_ref[pl.ds(c0, CHUNK), :]
        vc = v_ref[pl.ds(c0, CHUNK), :vw]
        out_rows = []
        for j in range(SUBLANES):
            seen = srow + c0 <= i0 + j
            decay = jnp.exp(jnp.where(seen, c8[j:j + 1, :] - cc, 0.0))
            w = jnp.where(seen, kc * decay * q8[j:j + 1, :], 0.0)
            scores = jnp.dot(w.astype(BF16), head_sum, preferred_element_type=F32)
            out_rows.append(jnp.sum(scores * vc, axis=0, keepdims=True))
        o_ref[pl.ds(i0, SUBLANES), :vw] = jnp.concatenate(out_rows, axis=0)
        return carry

    lax.fori_loop(0, t // SUBLANES, row_group, 0)
    return o_ref[:, :vw]


def _gated_chunks(q, k, v, cum, st, key_head_w, bd, rows):
    t = q.shape[0]
    intra_rows = None if rows is None else _intra_pairwise(q, k, v, cum, key_head_w, rows)
    outs = []
    for c in range(t // CHUNK):
        r = slice(c * CHUNK, (c + 1) * CHUNK)
        cm = cum[r]
        cl = cm[CHUNK - 1:CHUNK, :]
        vb = v[r].astype(BF16)
        if rows is None:
            intra, qe = _intra_matmul(q[r], k[r], vb, cm, key_head_w)
        else:
            intra, qe = intra_rows[r], q[r] * jnp.exp(cm)
        o = lax.dot_general(qe.astype(BF16), st.astype(BF16), NT_DIMS, preferred_element_type=F32)
        outs.append(o + intra)
        kl = (k[r] * jnp.exp(cl - cm)).astype(BF16)
        upd = lax.dot_general(vb, kl, TN_DIMS, preferred_element_type=F32)
        st = st * jnp.exp(cl) + jnp.where(bd, upd, 0.0)
    return jnp.concatenate(outs, axis=0), st


def _mixer_kernel(ok_ref, proj_ref, cos_ref, sin_ref, tri_ref, rdec_ref, xi_ref, zeta_ref, rcd_ref,
                  retn_ref, hgn_ref, lb_ref, glan_ref, ba_ref, wa2_ref,
                  o_ref, ret_st, hg_st, gl_st, *rows):
    @pl.when(pl.program_id(1) == 0)
    def _():
        ret_st[...] = jnp.zeros_like(ret_st)
        hg_st[...] = jnp.zeros_like(hg_st)
        gl_st[...] = jnp.zeros_like(gl_st)

    step = functools.partial(_mixer_step, proj_ref, cos_ref, sin_ref, tri_ref, rdec_ref, xi_ref, zeta_ref,
                             rcd_ref, retn_ref, hgn_ref, lb_ref, glan_ref, ba_ref, wa2_ref, o_ref,
                             ret_st, hg_st, gl_st)
    matmul_form_ok = ok_ref[pl.program_id(0) * pl.num_programs(1) + pl.program_id(1)] != 0

    @pl.when(matmul_form_ok)
    def _():
        step(None)

    @pl.when(jnp.logical_not(matmul_form_ok))
    def _():
        step(rows)


def _mixer_step(proj_ref, cos_ref, sin_ref, tri_ref, rdec_ref, xi_ref, zeta_ref, rcd_ref, retn_ref, hgn_ref,
                lb_ref, glan_ref, ba_ref, wa2_ref, o_ref, ret_st, hg_st, gl_st, rows):
    t = proj_ref.shape[1]
    lane = lax.broadcasted_iota(jnp.int32, (t, LANES), 1)
    sq_r = lax.broadcasted_iota(jnp.int32, (LANES, LANES), 0)
    sq_c = lax.broadcasted_iota(jnp.int32, (LANES, LANES), 1)
    bd = (sq_r // HEAD_DIM) == (sq_c // HEAD_DIM)
    bdmean = jnp.where(bd, 1.0 / HEAD_DIM, 0.0).astype(BF16)

    def cols(c0, w=LANES):
        return proj_ref[0, :, c0:c0 + w]

    cosb = cos_ref[0]
    sinb = sin_ref[0]
    first_half = (lane % HEAD_DIM) < (HEAD_DIM // 2)

    def rope(x):
        swapped = jnp.where(first_half, pltpu.roll(x, LANES - HEAD_DIM // 2, 1),
                            pltpu.roll(x, HEAD_DIM // 2, 1))
        return x * cosb + swapped * sinb

    for p in range(N_RET_PAIRS):
        o0 = p * LANES
        q = rope(cols(C_RQ + o0))
        k = rope(cols(C_RK + o0)) * (HEAD_DIM ** -0.5)
        kb = k.astype(BF16)
        vb = cols(C_RV + o0).astype(BF16)
        st = ret_st[p]
        o = lax.dot_general((q * xi_ref[p]).astype(BF16), st.astype(BF16), NT_DIMS,
                            preferred_element_type=F32)
        intra = None
        for hh in range(2):
            qm = jnp.where(lane // HEAD_DIM == hh, q, 0.0).astype(BF16)
            s = lax.dot_general(qm, kb, NT_DIMS, preferred_element_type=F32)
            pm = (s * rdec_ref[2 * p + hh]).astype(BF16)
            oh = jnp.dot(pm, vb, preferred_element_type=F32)
            intra = oh if intra is None else jnp.where(lane < HEAD_DIM, intra, oh)
        o = o + intra
        upd = lax.dot_general(vb, (k * zeta_ref[p]).astype(BF16), TN_DIMS, preferred_element_type=F32)
        ret_st[p] = st * rcd_ref[p] + jnp.where(bd, upd, 0.0)
        y = _head_norm_gate(o, bdmean, retn_ref[:, o0:o0 + LANES], cols(C_RG + o0))
        o_ref[0, :, o0:o0 + LANES] = y.astype(o_ref.dtype)

    log_gates = _log_gates(cols(C_HF, HGRN_W), cols(C_GA), lb_ref[...], wa2_ref[...], ba_ref[...])
    cum_all = _dot_exact_lhs(tri_ref[...], log_gates, 3)

    for p in range(N_HGRN_PAIRS):
        o0 = p * LANES
        lb = lb_ref[:, o0:o0 + LANES]
        k = (1.0 - lb) * jax.nn.sigmoid(-cols(C_HF + o0))
        q = _silu(cols(C_HQ + o0)) * (HEAD_DIM ** -0.5)
        o, st = _gated_chunks(q, k, cols(C_HI + o0), cum_all[:, o0:o0 + LANES], hg_st[p], HEAD_DIM, bd, rows)
        hg_st[p] = st
        y = _head_norm_gate(o, bdmean, hgn_ref[:, o0:o0 + LANES], cols(C_HG + o0))
        o_ref[0, :, RET_W + o0:RET_W + o0 + LANES] = y.astype(o_ref.dtype)

    cum = cum_all[:, HGRN_W:]
    ge = lax.broadcasted_iota(jnp.int32, (GLA_W, LANES), 0)
    gd = lax.broadcasted_iota(jnp.int32, (GLA_W, LANES), 1)
    bdg = (ge // HEAD_DIM) == (gd // GLA_DK)
    q = cols(C_GQ) * (GLA_DK ** -0.5)
    o, st = _gated_chunks(q, cols(C_GK), cols(C_GV, GLA_W), cum, gl_st[...], GLA_DK, bdg, rows)
    gl_st[...] = st
    for gi in range(GLA_W // LANES):
        o0 = gi * LANES
        y = _head_norm_gate(o[:, o0:o0 + LANES], bdmean, glan_ref[:, o0:o0 + LANES], cols(C_GG + o0))
        o_ref[0, :, RET_W + HGRN_W + o0:RET_W + HGRN_W + o0 + LANES] = y.astype(o_ref.dtype)


def _mixer_constants(t):
    idx = np.arange(t)
    tri = ((idx[:, None] >= idx[None, :]) & (idx[:, None] // CHUNK == idx[None, :] // CHUNK))
    log_gamma = jnp.log1p(-jnp.exp2(-5.0 - jnp.arange(RET_HEADS, dtype=F32)))
    i = jnp.arange(t, dtype=F32)
    rel = i[:, None] - i[None, :]
    rdec = jnp.where(rel >= 0, jnp.exp(log_gamma[:, None, None] * jnp.maximum(rel, 0.0)), 0.0)
    lane_head = jnp.repeat(log_gamma, HEAD_DIM).reshape(N_RET_PAIRS, 1, LANES)
    xi = jnp.exp(lane_head * (i[None, :, None] + 1.0))
    zeta = jnp.exp(lane_head * (t - 1.0 - i[None, :, None]))
    rcd = jnp.exp(lane_head * float(t))
    return jnp.asarray(tri, BF16), rdec, xi, zeta, rcd


def _mixer(matmul_form_ok, proj, cos_t, sin_t, consts, retn, hgn, lb, glan, ba, wa2p):
    b, s, _ = proj.shape
    t = MIX_TILE
    tri, rdec, xi, zeta, rcd = consts
    tile = lambda bi, j, ok: (bi, j, 0)
    const2 = lambda bi, j, ok: (0, 0)
    const3 = lambda bi, j, ok: (0, 0, 0)
    return pl.pallas_call(
        _mixer_kernel,
        grid_spec=pltpu.PrefetchScalarGridSpec(
            num_scalar_prefetch=1,
            grid=(b, s // t),
            in_specs=[pl.BlockSpec((1, t, PROJ_W), tile),
                      pl.BlockSpec((1, t, LANES), tile),
                      pl.BlockSpec((1, t, LANES), tile),
                      pl.BlockSpec((t, t), const2),
                      pl.BlockSpec((RET_HEADS, t, t), const3),
                      pl.BlockSpec((N_RET_PAIRS, t, LANES), const3),
                      pl.BlockSpec((N_RET_PAIRS, t, LANES), const3),
                      pl.BlockSpec((N_RET_PAIRS, 1, LANES), const3),
                      pl.BlockSpec((1, RET_W), const2),
                      pl.BlockSpec((1, HGRN_W), const2),
                      pl.BlockSpec((1, HGRN_W), const2),
                      pl.BlockSpec((1, GLA_W), const2),
                      pl.BlockSpec((1, GLA_QK), const2),
                      pl.BlockSpec((LANES, GLA_QK), const2)],
            out_specs=pl.BlockSpec((1, t, D_MODEL), tile),
            scratch_shapes=[pltpu.VMEM((N_RET_PAIRS, LANES, LANES), F32),
                            pltpu.VMEM((N_HGRN_PAIRS, LANES, LANES), F32),
                            pltpu.VMEM((GLA_W, LANES), F32),
                            pltpu.VMEM((t, LANES), F32), pltpu.VMEM((t, LANES), F32),
                            pltpu.VMEM((t, LANES), F32), pltpu.VMEM((t, GLA_W), F32),
                            pltpu.VMEM((t, GLA_W), F32)]),
        out_shape=jax.ShapeDtypeStruct((b, s, D_MODEL), BF16),
        compiler_params=_params("arbitrary", "arbitrary"),
        name="mixer",
    )(matmul_form_ok, proj, cos_t, sin_t, tri, rdec, xi, zeta, rcd, retn, hgn, lb, glan, ba, wa2p)


def _route(logits):
    lane = lax.broadcasted_iota(jnp.int32, logits.shape, 1)
    neg = jnp.float32(-jnp.inf)
    big = jnp.int32(LANES)
    is_g = (lane >= N_EXPERTS) & (lane < N_EXPERTS + N_GROUPS)
    gl = jnp.where(is_g, logits, neg)
    gmax = jnp.max(gl, axis=-1, keepdims=True)
    gidx = jnp.min(jnp.where(gl == gmax, lane - N_EXPERTS, big), axis=-1, keepdims=True)
    g_w = 1.0 / jnp.sum(jnp.where(is_g, jnp.exp(gl - gmax), 0.0), axis=-1, keepdims=True)
    in_grp = (lane < N_EXPERTS) & ((lane // EXPERTS_PER_GROUP) == gidx)
    el = jnp.where(in_grp, logits, neg)
    v1 = jnp.max(el, axis=-1, keepdims=True)
    i1 = jnp.min(jnp.where(in_grp & (el == v1), lane, big), axis=-1, keepdims=True)
    rest = in_grp & (lane != i1)
    el2 = jnp.where(rest, logits, neg)
    v2 = jnp.max(el2, axis=-1, keepdims=True)
    i2 = jnp.min(jnp.where(rest & (el2 == v2), lane, big), axis=-1, keepdims=True)
    e2 = jnp.exp(v2 - v1)
    w1 = g_w / (1.0 + e2)
    w2 = g_w * e2 / (1.0 + e2)
    return i1, i2, w1, w2


R_E1, R_E2, R_RANK1, R_RANK2, R_W1, R_W2 = 0, 1, 2, 3, 4, 5


def _outproj_kernel(m_ref, x_ref, mod_ref, g_ref, wo_ref, wr_ref, br_ref, tril_ref,
                    x1_ref, h2_ref, rinfo_ref, rfields_ref, counts_ref, carry_ref):
    @pl.when(pl.program_id(0) == 0)
    def _():
        carry_ref[...] = jnp.zeros_like(carry_ref)

    att = jnp.dot(m_ref[...], wo_ref[...], preferred_element_type=F32)
    x1 = x_ref[...] + mod_ref[0, 2:3, :] * att
    x1_ref[...] = x1
    h2 = _modulated_norm(x1, g_ref[...], mod_ref[0, 3:4, :], mod_ref[0, 4:5, :])
    _to_slabs(h2_ref, h2)
    h_hi = h2.astype(BF16)
    h_lo = (h2 - h_hi.astype(F32)).astype(BF16)
    both = jnp.dot(h_hi, wr_ref[...], preferred_element_type=F32)
    cross = jnp.dot(h_lo, wr_ref[:, :LANES], preferred_element_type=F32)
    logits = both[:, :LANES] + both[:, LANES:] + cross + br_ref[...]
    i1, i2, w1, w2 = _route(logits)
    lane = lax.broadcasted_iota(jnp.int32, logits.shape, 1)
    hit1 = lane == i1
    hit2 = lane == i2
    onehot = jnp.where(hit1 | hit2, 1.0, 0.0)
    before = carry_ref[...] + jnp.dot(tril_ref[...], onehot.astype(BF16), preferred_element_type=F32)
    r1 = jnp.sum(jnp.where(hit1, before, 0.0), axis=-1, keepdims=True)
    r2 = jnp.sum(jnp.where(hit2, before, 0.0), axis=-1, keepdims=True)
    carry_ref[...] += jnp.sum(onehot, axis=0, keepdims=True)
    counts_ref[...] = carry_ref[...]
    rec = jnp.zeros(logits.shape, F32)
    for pos, val in ((R_E1, i1.astype(F32)), (R_E2, i2.astype(F32)), (R_RANK1, r1), (R_RANK2, r2),
                     (R_W1, w1), (R_W2, w2)):
        rec = jnp.where(lane == pos, val, rec)
    rinfo_ref[...] = rec
    rfields_ref[...] = rec.T[:SUBLANES]


def _outproj(merged2d, x2d, mod, gain, w_out, w_r, b_r, seq):
    n, d = x2d.shape
    tm = ROW_TILE
    per_b = seq // tm
    row = lambda i: (i, 0)
    const2 = lambda i: (0, 0)
    idx = np.arange(tm)
    tril = jnp.asarray(idx[:, None] > idx[None, :], BF16)
    return pl.pallas_call(
        _outproj_kernel,
        grid=(n // tm,),
        in_specs=[pl.BlockSpec((tm, d), row),
                  pl.BlockSpec((tm, d), row),
                  pl.BlockSpec((1, N_MOD, d), lambda i: (i // per_b, 0, 0)),
                  pl.BlockSpec((1, d), const2),
                  pl.BlockSpec((d, d), const2),
                  pl.BlockSpec((d, 2 * LANES), const2),
                  pl.BlockSpec((1, LANES), const2),
                  pl.BlockSpec((tm, tm), const2)],
        out_specs=[pl.BlockSpec((tm, d), row),
                   pl.BlockSpec((tm * F32_CHUNKS, LANES), row),
                   pl.BlockSpec((tm, LANES), row),
                   pl.BlockSpec((SUBLANES, tm), lambda i: (0, i)),
                   pl.BlockSpec((1, LANES), const2)],
        out_shape=[jax.ShapeDtypeStruct((n, d), F32),
                   jax.ShapeDtypeStruct((n * F32_CHUNKS, LANES), F32),
                   jax.ShapeDtypeStruct((n, LANES), F32),
                   jax.ShapeDtypeStruct((SUBLANES, n), F32),
                   jax.ShapeDtypeStruct((1, LANES), F32)],
        scratch_shapes=[pltpu.VMEM((1, LANES), F32)],
        compiler_params=_params("arbitrary"),
        name="outproj_route",
    )(merged2d, x2d, mod, gain, w_out, w_r, b_r, tril)


def _slot_layout(rfields, counts, n_tiles):
    cnt = counts[0, :N_EXPERTS].astype(jnp.int32)
    padded = (cnt + EXPERT_TILE - 1) // EXPERT_TILE * EXPERT_TILE
    ends = jnp.cumsum(padded)
    starts = ends - padded
    e = rfields[R_E1:R_E2 + 1].astype(jnp.int32)
    rank = rfields[R_RANK1:R_RANK2 + 1].astype(jnp.int32)
    ids = jnp.arange(N_EXPERTS, dtype=jnp.int32)[:, None, None]
    start_of = jnp.sum(jnp.where(e[None] == ids, starts[:, None, None], 0), axis=0)
    pos = (start_of + rank).reshape(-1)
    tile_ids = jnp.arange(n_tiles, dtype=jnp.int32)
    tile_start = tile_ids * EXPERT_TILE
    tile_expert = jnp.sum((tile_start[:, None] >= ends[None, :]).astype(jnp.int32), axis=-1)
    tile_expert = jnp.minimum(tile_expert, N_EXPERTS - 1)
    tile_valid = (tile_start < ends[-1]).astype(jnp.int32)
    used_tiles = ends[-1:] // EXPERT_TILE
    tile_src = jnp.minimum(tile_ids, used_tiles - 1)
    return pos, starts + cnt, padded - cnt, used_tiles, tile_expert, tile_valid, tile_src


def _row_copy(src_ref, src_tok, dst_ref, dst_tok, chunks, sem):
    src = src_ref.at[pl.ds(pl.multiple_of(src_tok * chunks, chunks), chunks)]
    dst = dst_ref.at[pl.ds(pl.multiple_of(dst_tok * chunks, chunks), chunks)]
    return pltpu.make_async_copy(src, dst, sem)


def _to_slabs(ref, val):
    t = val.shape[0]
    chunks = ref.shape[0] // t
    for j in range(chunks):
        ref[pl.ds(j, t, stride=chunks), :] = val[:, j * LANES:(j + 1) * LANES]


def _from_slabs(ref, chunks):
    t = ref.shape[0] // chunks
    return jnp.concatenate([ref[pl.ds(j, t, stride=chunks), :] for j in range(chunks)], axis=1)


def _pad_copy(zbuf, xs_ref, pad_start, pad_len, bit, sem):
    rows = (1 << bit) * F32_CHUNKS
    slot = pad_start + (pad_len & ((1 << bit) - 1))
    dst = xs_ref.at[pl.ds(pl.multiple_of(slot * F32_CHUNKS, F32_CHUNKS), rows)]
    return pltpu.make_async_copy(zbuf.at[pl.ds(0, rows)], dst, sem)


def _dispatch_kernel(pos_ref, pstart_ref, plen_ref, used_ref, h_ref, xs_ref, zbuf, sem, zsem):
    @pl.when(pl.program_id(0) == 0)
    def _():
        zbuf[...] = jnp.zeros_like(zbuf)
        tile_rows = EXPERT_TILE * F32_CHUNKS

        def each(e, carry, wait):
            for bit in range(PAD_BITS):
                @pl.when(((plen_ref[e] >> bit) & 1) == 1)
                def _():
                    copy = _pad_copy(zbuf, xs_ref, pstart_ref[e], plen_ref[e], bit, zsem)
                    copy.wait() if wait else copy.start()
            return carry

        def tail(j, carry, wait):
            dst = xs_ref.at[pl.ds(pl.multiple_of(j * tile_rows, tile_rows), tile_rows)]
            copy = pltpu.make_async_copy(zbuf, dst, zsem)
            copy.wait() if wait else copy.start()
            return carry

        n_tiles = xs_ref.shape[0] // tile_rows
        for wait in (False, True):
            lax.fori_loop(0, N_EXPERTS, functools.partial(each, wait=wait), 0)
            lax.fori_loop(used_ref[0], n_tiles, functools.partial(tail, wait=wait), 0)

    td = h_ref.shape[0] // F32_CHUNKS
    base = pl.program_id(0) * td
    n_tok = pos_ref.shape[0] // 2

    def issue(t, carry):
        for k in range(2):
            _row_copy(h_ref, t, xs_ref, pos_ref[k * n_tok + base + t], F32_CHUNKS, sem).start(priority=k)
        return carry

    lax.fori_loop(0, td, issue, 0, unroll=8)

    def drain(t, carry):
        for k in range(2):
            _row_copy(h_ref, 0, xs_ref, 0, F32_CHUNKS, sem).wait()
        return carry

    lax.fori_loop(0, td, drain, 0, unroll=8)


def _dispatch(pos, pad_start, pad_len, used_tiles, h2, n_slots):
    n = h2.shape[0] // F32_CHUNKS
    td = min(DISPATCH_TILE, n)
    return pl.pallas_call(
        _dispatch_kernel,
        grid_spec=pltpu.PrefetchScalarGridSpec(
            num_scalar_prefetch=4,
            grid=(n // td,),
            in_specs=[pl.BlockSpec((td * F32_CHUNKS, LANES), lambda i, *_: (i, 0))],
            out_specs=pl.BlockSpec(memory_space=pl.ANY),
            scratch_shapes=[pltpu.VMEM((EXPERT_TILE * F32_CHUNKS, LANES), F32),
                            pltpu.SemaphoreType.DMA(()),
                            pltpu.SemaphoreType.DMA(())]),
        out_shape=jax.ShapeDtypeStruct((n_slots * F32_CHUNKS, LANES), h2.dtype),
        compiler_params=_params("arbitrary"),
        name="moe_dispatch",
    )(pos, pad_start, pad_len, used_tiles, h2)


def _expert_kernel(te_ref, tv_ref, ts_ref, xs_ref, wg_ref, wu_ref, wd_ref, ys_ref, wgb, wub, wdb):
    del ts_ref
    i = pl.program_id(0)
    new_expert = jnp.logical_or(i == 0, te_ref[i] != te_ref[jnp.maximum(i - 1, 0)])

    @pl.when(new_expert)
    def _():
        wgb[...] = wg_ref[0, 0].astype(BF16)
        wub[...] = wu_ref[0, 0].astype(BF16)
        wdb[...] = wd_ref[0, 0].astype(BF16)

    @pl.when(tv_ref[i] != 0)
    def _():
        x = _from_slabs(xs_ref, F32_CHUNKS).astype(BF16)
        a = jnp.dot(x, wgb[...], preferred_element_type=F32)
        b = jnp.dot(x, wub[...], preferred_element_type=F32)
        _to_slabs(ys_ref, jnp.dot((_silu(a) * b).astype(BF16), wdb[...], preferred_element_type=F32))

    @pl.when(tv_ref[i] == 0)
    def _():
        ys_ref[...] = jnp.zeros_like(ys_ref)


def _experts(tile_expert, tile_valid, tile_src, xs, wg, wu, wd, layer):
    n_slots = xs.shape[0] // F32_CHUNKS
    d = D_MODEL
    tm = EXPERT_TILE
    by_expert = lambda i, te, tv, ts: (layer, te[i], 0, 0)
    return pl.pallas_call(
        _expert_kernel,
        grid_spec=pltpu.PrefetchScalarGridSpec(
            num_scalar_prefetch=3,
            grid=(n_slots // tm,),
            in_specs=[pl.BlockSpec((tm * F32_CHUNKS, LANES), lambda i, te, tv, ts: (ts[i], 0)),
                      pl.BlockSpec((1, 1, d, D_EXPERT), by_expert),
                      pl.BlockSpec((1, 1, d, D_EXPERT), by_expert),
                      pl.BlockSpec((1, 1, D_EXPERT, d), by_expert)],
            out_specs=pl.BlockSpec((tm * F32_CHUNKS, LANES), lambda i, te, tv, ts: (i, 0)),
            scratch_shapes=[pltpu.VMEM((d, D_EXPERT), BF16),
                            pltpu.VMEM((d, D_EXPERT), BF16),
                            pltpu.VMEM((D_EXPERT, d), BF16)]),
        out_shape=jax.ShapeDtypeStruct((n_slots * F32_CHUNKS, LANES), F32),
        compiler_params=_params("arbitrary"),
        name="moe_experts",
    )(tile_expert, tile_valid, tile_src, xs, wg, wu, wd)


def _combine_kernel(pos_ref, x1_ref, rinfo_ref, mod_ref, nf_ref, ys_ref, o_ref, buf, sem, *, final):
    i = pl.program_id(0)
    tc = x1_ref.shape[0]
    n_tok = pos_ref.shape[0] // 2

    def issue(tile, slot):
        base = tile * tc

        def body(t, carry):
            for k in range(2):
                _row_copy(ys_ref, pos_ref[k * n_tok + base + t], buf.at[slot, k], t, F32_CHUNKS,
                          sem.at[slot]).start(priority=k)
            return carry

        lax.fori_loop(0, tc, body, 0, unroll=8)

    @pl.when(i == 0)
    def _():
        issue(0, 0)

    @pl.when(i + 1 < pl.num_programs(0))
    def _():
        issue(i + 1, (i + 1) % 2)

    slot = i % 2

    def drain(t, carry):
        for k in range(2):
            _row_copy(ys_ref, 0, buf.at[slot, k], 0, F32_CHUNKS, sem.at[slot]).wait()
        return carry

    lax.fori_loop(0, tc, drain, 0, unroll=8)

    rinfo = rinfo_ref[...]
    y = (rinfo[:, R_W1:R_W1 + 1] * _from_slabs(buf.at[slot, 0], F32_CHUNKS)
         + rinfo[:, R_W2:R_W2 + 1] * _from_slabs(buf.at[slot, 1], F32_CHUNKS))
    x2 = x1_ref[...] + mod_ref[0, 5:6, :] * y
    if final:
        ms = jnp.mean(x2 * x2, axis=-1, keepdims=True)
        x2 = x2 * lax.rsqrt(ms + EPS) * nf_ref[...]
    o_ref[...] = x2


def _combine(pos, x1, rinfo, mod, nf, ys, seq, final):
    n, d = x1.shape
    tc = COMBINE_TILE
    per_b = seq // tc
    row = lambda i, pos: (i, 0)
    return pl.pallas_call(
        functools.partial(_combine_kernel, final=final),
        grid_spec=pltpu.PrefetchScalarGridSpec(
            num_scalar_prefetch=1,
            grid=(n // tc,),
            in_specs=[pl.BlockSpec((tc, d), row),
                      pl.BlockSpec((tc, LANES), row),
                      pl.BlockSpec((1, N_MOD, d), lambda i, pos: (i // per_b, 0, 0)),
                      pl.BlockSpec((1, d), lambda i, pos: (0, 0)),
                      pl.BlockSpec(memory_space=pl.ANY)],
            out_specs=pl.BlockSpec((tc, d), row),
            scratch_shapes=[pltpu.VMEM((2, 2, tc * F32_CHUNKS, LANES), F32),
                            pltpu.SemaphoreType.DMA((2,))]),
        out_shape=jax.ShapeDtypeStruct((n, d), F32),
        compiler_params=_params("arbitrary"),
        name="moe_combine",
    )(pos, x1, rinfo, mod, nf, ys)


def kernel(x, c, positions, w_ada, b_ada, norm_mix, norm_ffn, w_in, ret_norm, hgrn_norm, hgrn_lb_logits,
           gla_wa2, gla_ba, gla_norm, w_out, router_group_w, router_group_b, router_expert_w,
           router_expert_b, expert_w_gate, expert_w_up, expert_w_down, norm_final):
    b, s, d = x.shape
    depth = w_ada.shape[0]
    n = b * s
    assert d == D_MODEL and s % ROW_TILE == 0 and s % MIX_TILE == 0

    inv_freq = ROPE_BASE ** (-jnp.arange(0, HEAD_DIM, 2, dtype=F32) / HEAD_DIM)
    half = HEAD_DIM // 2
    ang = positions.astype(F32)[..., None] * jnp.tile(inv_freq, LANES // half)
    sign = jnp.where((jnp.arange(LANES) // half) % 2 == 0, -1.0, 1.0).astype(F32)
    cos_t = jnp.cos(ang)
    sin_t = jnp.sin(ang) * sign

    lb_w = jax.nn.softmax(hgrn_lb_logits.astype(F32), axis=0)
    lower_bounds = jnp.cumsum(lb_w, axis=0) - lb_w[0]

    consts = _mixer_constants(MIX_TILE)
    mod_all = _ada_mod(c, w_ada, b_ada).reshape(depth, b, N_MOD, d)

    w_in_pad = jnp.pad(w_in, ((0, 0), (0, 0), (0, PROJ_W - IN_PROJ_DIM))).astype(BF16)
    wa2_pad = jnp.pad(gla_wa2, ((0, 0), (0, LANES - GLA_RANK), (0, 0)))
    w_out_b = w_out.astype(BF16)
    pad_r = LANES - N_EXPERTS - N_GROUPS
    w_r = jnp.pad(jnp.concatenate([router_expert_w, router_group_w], axis=-1), ((0, 0), (0, 0), (0, pad_r)))
    w_r_hi = w_r.astype(BF16)
    w_r = jnp.concatenate([w_r_hi, (w_r - w_r_hi.astype(F32)).astype(BF16)], axis=-1)
    b_r = jnp.pad(jnp.concatenate([router_expert_b, router_group_b], axis=-1), ((0, 0), (0, pad_r)))
    n_slots = (2 * n // EXPERT_TILE + N_EXPERTS) * EXPERT_TILE

    x2d = x.reshape(n, d)
    for l in range(depth):
        mod = mod_all[l]
        lb_l = lower_bounds[l].reshape(1, HGRN_W)
        ba_l = gla_ba[l].reshape(1, GLA_QK)
        proj, sub_decay = _inproj(x2d, mod, norm_mix[l].reshape(1, d), w_in_pad[l], lb_l, wa2_pad[l], ba_l, s)
        matmul_form_ok = (sub_decay[:, 0, :ROW_TILE // MIX_TILE].reshape(-1) > -MAX_SUB_DECAY).astype(jnp.int32)
        merged = _mixer(matmul_form_ok, proj.reshape(b, s, PROJ_W), cos_t, sin_t, consts,
                        ret_norm[l].reshape(1, RET_W), hgrn_norm[l].reshape(1, HGRN_W),
                        lower_bounds[l].reshape(1, HGRN_W), gla_norm[l].reshape(1, GLA_W),
                        gla_ba[l].reshape(1, GLA_QK), wa2_pad[l])
        x1, h2p, rinfo, rfields, counts = _outproj(merged.reshape(n, d), x2d, mod, norm_ffn[l].reshape(1, d),
                                                   w_out_b[l], w_r[l], b_r[l].reshape(1, LANES), s)
        pos, pad_start, pad_len, used_tiles, tile_expert, tile_valid, tile_src = _slot_layout(
            rfields, counts, n_slots // EXPERT_TILE)
        xs = _dispatch(pos, pad_start, pad_len, used_tiles, h2p, n_slots)
        ys = _experts(tile_expert, tile_valid, tile_src, xs, expert_w_gate, expert_w_up, expert_w_down, l)
        x2d = _combine(pos, x1, rinfo, mod, norm_final.reshape(1, d), ys, s, final=(l == depth - 1))
    return x2d.reshape(b, s, d)
```

```python
import functools

import jax
import jax.numpy as jnp
import numpy as np
from jax import lax
from jax.experimental import pallas as pl
from jax.experimental.pallas import tpu as pltpu

F32 = jnp.float32
BF16 = jnp.bfloat16
HIGHEST = lax.Precision.HIGHEST

D_MODEL = 1024
HEAD_DIM = 64
LANES = 128
SUBLANES = 8
RET_W = 384
HGRN_W = 384
GLA_W = 256
GLA_QK = 128
GLA_DK = 32
GLA_RANK = 16
GLA_TAU = 16.0
CHUNK = 64
SUB = 32
MAX_SUB_DECAY = 80.0
ROPE_BASE = 10000.0
EPS = 1e-6
F_MIN = 1e-30
N_GROUPS = 4
EXPERTS_PER_GROUP = 8
N_EXPERTS = 32
D_EXPERT = 256
N_MOD = 6
RET_HEADS = RET_W // HEAD_DIM
N_RET_PAIRS = RET_W // LANES
N_HGRN_PAIRS = HGRN_W // LANES

C_RQ, C_RK, C_RV, C_RG = 0, 384, 768, 1152
C_HQ, C_HF, C_HI, C_HG = 1536, 1920, 2304, 2688
C_GQ, C_GK, C_GV, C_GG, C_GA = 3072, 3200, 3328, 3584, 3840
IN_PROJ_DIM = 3856
PROJ_W = C_GA + LANES

ROW_TILE = 512
MIX_TILE = 256
EXPERT_TILE = 512
PAD_BITS = EXPERT_TILE.bit_length() - 1
DISPATCH_TILE = 512
COMBINE_TILE = 256
F32_CHUNKS = D_MODEL // LANES
VMEM_LIMIT = 56 * 1024 * 1024

NT_DIMS = (((1,), (1,)), ((), ()))
TN_DIMS = (((0,), (0,)), ((), ()))


def _silu(x):
    return x * jax.nn.sigmoid(x)


def _params(*sem):
    return pltpu.CompilerParams(dimension_semantics=sem, vmem_limit_bytes=VMEM_LIMIT)


def _ada_kernel(c_ref, w_ref, b_ref, o_ref):
    ca = _silu(c_ref[...])
    o_ref[0] = jnp.dot(ca, w_ref[0], precision=HIGHEST, preferred_element_type=F32) + b_ref[0]


def _ada_mod(c, w_ada, b_ada):
    depth, d, n = w_ada.shape
    b = c.shape[0]
    tn = 1536
    return pl.pallas_call(
        _ada_kernel,
        grid=(depth, n // tn),
        in_specs=[pl.BlockSpec((b, d), lambda l, j: (0, 0)),
                  pl.BlockSpec((1, d, tn), lambda l, j: (l, 0, j)),
                  pl.BlockSpec((1, 1, tn), lambda l, j: (l, 0, j))],
        out_specs=pl.BlockSpec((1, b, tn), lambda l, j: (l, 0, j)),
        out_shape=jax.ShapeDtypeStruct((depth, b, n), F32),
        compiler_params=_params("arbitrary", "arbitrary"),
        name="ada_mod",
    )(c, w_ada, b_ada.reshape(depth, 1, n))


def _modulated_norm(x, gain, shift, scale):
    ms = jnp.mean(x * x, axis=-1, keepdims=True)
    return (x * lax.rsqrt(ms + EPS)) * gain * (1.0 + scale) + shift


def _log_gates(hf, ga, lb, wa2, ba):
    f = lb + (1.0 - lb) * jax.nn.sigmoid(hf)
    u = _dot_bf16x3(ga, wa2) + ba
    log_a = (jnp.minimum(u, 0.0) - jnp.log(1.0 + jnp.exp(-jnp.abs(u)))) * (1.0 / GLA_TAU)
    return jnp.concatenate([jnp.log(jnp.maximum(f, F_MIN)), log_a], axis=1)


def _inproj_kernel(x_ref, mod_ref, g_ref, w_ref, lb_ref, wa2_ref, ba_ref, blk_ref, o_ref, dec_ref):
    h = _modulated_norm(x_ref[...], g_ref[...], mod_ref[0, 0:1, :], mod_ref[0, 1:2, :])
    proj = jnp.dot(h.astype(BF16), w_ref[...], preferred_element_type=F32)
    o_ref[...] = proj
    logs = _log_gates(proj[:, C_HF:C_HF + HGRN_W], proj[:, C_GA:C_GA + LANES], lb_ref[...], wa2_ref[...],
                      ba_ref[...])
    o_ref[:, C_HF:C_HF + HGRN_W] = logs[:, :HGRN_W]
    o_ref[:, C_GA:C_GA + LANES] = logs[:, HGRN_W:]
    sums = jnp.min(_dot_exact_lhs(blk_ref[...], logs, 2), axis=1, keepdims=True)
    per_tile = MIX_TILE // SUB
    lane = lax.broadcasted_iota(jnp.int32, (1, LANES), 1)
    rec = jnp.zeros((1, LANES), F32)
    for j in range(sums.shape[0] // per_tile):
        rec = jnp.where(lane == j, jnp.min(sums[j * per_tile:(j + 1) * per_tile]), rec)
    dec_ref[0] = rec


def _inproj(x2d, mod, gain, w_pad, lb, wa2p, ba, seq):
    n, d = x2d.shape
    tm = ROW_TILE
    per_b = seq // tm
    rows = np.arange(tm)
    blk = jnp.asarray(rows[None, :] // SUB == np.arange(tm // SUB)[:, None], BF16)
    const2 = lambda i: (0, 0)
    return pl.pallas_call(
        _inproj_kernel,
        grid=(n // tm,),
        in_specs=[pl.BlockSpec((tm, d), lambda i: (i, 0)),
                  pl.BlockSpec((1, N_MOD, d), lambda i: (i // per_b, 0, 0)),
                  pl.BlockSpec((1, d), const2),
                  pl.BlockSpec((d, PROJ_W), const2),
                  pl.BlockSpec((1, HGRN_W), const2),
                  pl.BlockSpec((LANES, GLA_QK), const2),
                  pl.BlockSpec((1, GLA_QK), const2),
                  pl.BlockSpec((tm // SUB, tm), const2)],
        out_specs=[pl.BlockSpec((tm, PROJ_W), lambda i: (i, 0)),
                   pl.BlockSpec((1, 1, LANES), lambda i: (i, 0, 0))],
        out_shape=[jax.ShapeDtypeStruct((n, PROJ_W), F32),
                   jax.ShapeDtypeStruct((n // tm, 1, LANES), F32)],
        compiler_params=_params("arbitrary"),
        name="norm_inproj",
    )(x2d, mod, gain, w_pad, lb, wa2p, ba, blk)


def _bf16_terms(x, n_terms):
    terms = []
    for _ in range(n_terms - 1):
        t = x.astype(BF16)
        terms.append(t)
        x = x - t.astype(F32)
    terms.append(x.astype(BF16))
    return terms


def _dot_exact_lhs(a_bf16, x, n_terms):
    return sum(jnp.dot(a_bf16, t, preferred_element_type=F32) for t in _bf16_terms(x, n_terms))


def _dot_exact_rhs(x, b_bf16, n_terms):
    return sum(jnp.dot(t, b_bf16, preferred_element_type=F32) for t in _bf16_terms(x, n_terms))


def _dot_bf16x3(a, b):
    a_hi, a_lo = _bf16_terms(a, 2)
    b_hi, b_lo = _bf16_terms(b, 2)
    return (jnp.dot(a_hi, b_hi, preferred_element_type=F32) + jnp.dot(a_lo, b_hi, preferred_element_type=F32)
            + jnp.dot(a_hi, b_lo, preferred_element_type=F32))


def _head_norm_gate(o, bdmean, gain, g):
    ms = _dot_exact_rhs(o * o, bdmean, 1)
    return o * lax.rsqrt(ms + EPS) * gain * _silu(g)


def _sub_bounds(cm):
    return [jnp.zeros((1, cm.shape[1]), F32)] + [cm[i * SUB - 1:i * SUB, :] for i in range(1, CHUNK // SUB)]


def _intra_matmul(qc, kc, vb, cm, key_head_w):
    vw = vb.shape[1]
    n_heads = LANES // key_head_w
    n_sub = CHUNK // SUB
    klane = lax.broadcasted_iota(jnp.int32, (CHUNK, LANES), 1)
    krow = lax.broadcasted_iota(jnp.int32, (CHUNK, LANES), 0)
    srow = lax.broadcasted_iota(jnp.int32, (n_heads * CHUNK, CHUNK), 0) % CHUNK
    scol = lax.broadcasted_iota(jnp.int32, (n_heads * CHUNK, CHUNK), 1)
    causal = srow >= scol
    hlane = jnp.concatenate([klane] * n_sub, axis=1) // key_head_w
    bounds = _sub_bounds(cm)
    bmat = jnp.concatenate([jnp.broadcast_to(bi, (SUB, LANES)) for bi in bounds], axis=0)
    ebmat = jnp.concatenate([jnp.broadcast_to(jnp.exp(bi), (SUB, LANES)) for bi in bounds], axis=0)
    qt = qc * jnp.exp(cm - bmat)
    lhs = jnp.concatenate([jnp.where(krow // SUB == i, qt, 0.0) for i in range(n_sub)], axis=1)
    kparts = []
    for i in range(n_sub):
        hi = (i + 1) * SUB
        ki = kc[:hi] * jnp.exp(bounds[i] - cm[:hi])
        if hi < CHUNK:
            ki = jnp.concatenate([ki, jnp.zeros((CHUNK - hi, LANES), F32)], axis=0)
        kparts.append(ki)
    kstack = jnp.concatenate(kparts, axis=1).astype(BF16)
    lhs_h = jnp.concatenate([jnp.where(hlane == h, lhs, 0.0) for h in range(n_heads)],
                            axis=0).astype(BF16)
    s = lax.dot_general(lhs_h, kstack, NT_DIMS, preferred_element_type=F32)
    pm = jnp.where(causal, s, 0.0).astype(BF16)
    pv = jnp.dot(pm, vb, preferred_element_type=F32)
    groups = []
    for gi in range(vw // LANES):
        h0 = gi * (LANES // HEAD_DIM)
        top = pv[h0 * CHUNK:(h0 + 1) * CHUNK, gi * LANES:(gi + 1) * LANES]
        bot = pv[(h0 + 1) * CHUNK:(h0 + 2) * CHUNK, gi * LANES:(gi + 1) * LANES]
        groups.append(jnp.where(klane < HEAD_DIM, top, bot))
    intra = groups[0] if len(groups) == 1 else jnp.concatenate(groups, axis=1)
    return intra, qt * ebmat


def _intra_pairwise(q, k, v, cum, key_head_w, rows):
    q_ref, k_ref, c_ref, v_ref, o_ref = rows
    t = q.shape[0]
    vw = v.shape[1]
    q_ref[...] = q
    k_ref[...] = k
    c_ref[...] = cum
    v_ref[:, :vw] = v
    d_head = lax.broadcasted_iota(jnp.int32, (LANES, vw), 0) // key_head_w
    e_head = lax.broadcasted_iota(jnp.int32, (LANES, vw), 1) // HEAD_DIM
    head_sum = jnp.where(d_head == e_head, 1.0, 0.0).astype(BF16)
    srow = lax.broadcasted_iota(jnp.int32, (CHUNK, LANES), 0)

    def row_group(g, carry):
        i0 = pl.multiple_of(g * SUBLANES, SUBLANES)
        c0 = pl.multiple_of(i0 // CHUNK * CHUNK, CHUNK)
        q8 = q_ref[pl.ds(i0, SUBLANES), :]
        c8 = c_ref[pl.ds(i0, SUBLANES), :]
        kc = k_ref[pl.ds(c0, CHUNK), :]
        cc = c_ref[pl.ds(c0, CHUNK), :]
        vc = v_ref[pl.ds(c0, CHUNK), :vw]
        out_rows = []
        for j in range(SUBLANES):
            seen = srow + c0 <= i0 + j
            decay = jnp.exp(jnp.where(seen, c8[j:j + 1, :] - cc, 0.0))
            w = jnp.where(seen, kc * decay * q8[j:j + 1, :], 0.0)
            scores = jnp.dot(w.astype(BF16), head_sum, preferred_element_type=F32)
            out_rows.append(jnp.sum(scores * vc, axis=0, keepdims=True))
        o_ref[pl.ds(i0, SUBLANES), :vw] = jnp.concatenate(out_rows, axis=0)
        return carry

    lax.fori_loop(0, t // SUBLANES, row_group, 0)
    return o_ref[:, :vw]


def _gated_chunks(q, k, v, cum, st, key_head_w, bd, rows):
    t = q.shape[0]
    intra_rows = None if rows is None else _intra_pairwise(q, k, v, cum, key_head_w, rows)
    outs = []
    for c in range(t // CHUNK):
        r = slice(c * CHUNK, (c + 1) * CHUNK)
        cm = cum[r]
        cl = cm[CHUNK - 1:CHUNK, :]
        vb = v[r].astype(BF16)
        if rows is None:
            intra, qe = _intra_matmul(q[r], k[r], vb, cm, key_head_w)
        else:
            intra, qe = intra_rows[r], q[r] * jnp.exp(cm)
        o = lax.dot_general(qe.astype(BF16), st.astype(BF16), NT_DIMS, preferred_element_type=F32)
        outs.append(o + intra)
        kl = (k[r] * jnp.exp(cl - cm)).astype(BF16)
        upd = lax.dot_general(vb, kl, TN_DIMS, preferred_element_type=F32)
        st = st * jnp.exp(cl) + jnp.where(bd, upd, 0.0)
    return jnp.concatenate(outs, axis=0), st


def _mixer_kernel(ok_ref, proj_ref, cos_ref, sin_ref, tri_ref, rdec_ref, xi_ref, zeta_ref, rcd_ref,
                  retn_ref, hgn_ref, lb_ref, glan_ref, ba_ref, wa2_ref,
                  o_ref, ret_st, hg_st, gl_st, *rows):
    @pl.when(pl.program_id(1) == 0)
    def _():
        ret_st[...] = jnp.zeros_like(ret_st)
        hg_st[...] = jnp.zeros_like(hg_st)
        gl_st[...] = jnp.zeros_like(gl_st)

    step = functools.partial(_mixer_step, proj_ref, cos_ref, sin_ref, tri_ref, rdec_ref, xi_ref, zeta_ref,
                             rcd_ref, retn_ref, hgn_ref, lb_ref, glan_ref, ba_ref, wa2_ref, o_ref,
                             ret_st, hg_st, gl_st)
    matmul_form_ok = ok_ref[pl.program_id(0) * pl.num_programs(1) + pl.program_id(1)] != 0

    @pl.when(matmul_form_ok)
    def _():
        step(None)

    @pl.when(jnp.logical_not(matmul_form_ok))
    def _():
        step(rows)


def _mixer_step(proj_ref, cos_ref, sin_ref, tri_ref, rdec_ref, xi_ref, zeta_ref, rcd_ref, retn_ref, hgn_ref,
                lb_ref, glan_ref, ba_ref, wa2_ref, o_ref, ret_st, hg_st, gl_st, rows):
    t = proj_ref.shape[1]
    lane = lax.broadcasted_iota(jnp.int32, (t, LANES), 1)
    sq_r = lax.broadcasted_iota(jnp.int32, (LANES, LANES), 0)
    sq_c = lax.broadcasted_iota(jnp.int32, (LANES, LANES), 1)
    bd = (sq_r // HEAD_DIM) == (sq_c // HEAD_DIM)
    bdmean = jnp.where(bd, 1.0 / HEAD_DIM, 0.0).astype(BF16)

    def cols(c0, w=LANES):
        return proj_ref[0, :, c0:c0 + w]

    cosb = cos_ref[0]
    sinb = sin_ref[0]
    first_half = (lane % HEAD_DIM) < (HEAD_DIM // 2)

    def rope(x):
        swapped = jnp.where(first_half, pltpu.roll(x, LANES - HEAD_DIM // 2, 1),
                            pltpu.roll(x, HEAD_DIM // 2, 1))
        return x * cosb + swapped * sinb

    for p in range(N_RET_PAIRS):
        o0 = p * LANES
        q = rope(cols(C_RQ + o0))
        k = rope(cols(C_RK + o0)) * (HEAD_DIM ** -0.5)
        kb = k.astype(BF16)
        vb = cols(C_RV + o0).astype(BF16)
        st = ret_st[p]
        o = lax.dot_general((q * xi_ref[p]).astype(BF16), st.astype(BF16), NT_DIMS,
                            preferred_element_type=F32)
        intra = None
        for hh in range(2):
            qm = jnp.where(lane // HEAD_DIM == hh, q, 0.0).astype(BF16)
            s = lax.dot_general(qm, kb, NT_DIMS, preferred_element_type=F32)
            pm = (s * rdec_ref[2 * p + hh]).astype(BF16)
            oh = jnp.dot(pm, vb, preferred_element_type=F32)
            intra = oh if intra is None else jnp.where(lane < HEAD_DIM, intra, oh)
        o = o + intra
        upd = lax.dot_general(vb, (k * zeta_ref[p]).astype(BF16), TN_DIMS, preferred_element_type=F32)
        ret_st[p] = st * rcd_ref[p] + jnp.where(bd, upd, 0.0)
        y = _head_norm_gate(o, bdmean, retn_ref[:, o0:o0 + LANES], cols(C_RG + o0))
        o_ref[0, :, o0:o0 + LANES] = y.astype(o_ref.dtype)

    log_gates = jnp.concatenate([cols(C_HF, HGRN_W), cols(C_GA)], axis=1)
    cum_all = _dot_exact_lhs(tri_ref[...], log_gates, 3)

    for p in range(N_HGRN_PAIRS):
        o0 = p * LANES
        k = 1.0 - jnp.exp(cols(C_HF + o0))
        q = _silu(cols(C_HQ + o0)) * (HEAD_DIM ** -0.5)
        o, st = _gated_chunks(q, k, cols(C_HI + o0), cum_all[:, o0:o0 + LANES], hg_st[p], HEAD_DIM, bd, rows)
        hg_st[p] = st
        y = _head_norm_gate(o, bdmean, hgn_ref[:, o0:o0 + LANES], cols(C_HG + o0))
        o_ref[0, :, RET_W + o0:RET_W + o0 + LANES] = y.astype(o_ref.dtype)

    cum = cum_all[:, HGRN_W:]
    ge = lax.broadcasted_iota(jnp.int32, (GLA_W, LANES), 0)
    gd = lax.broadcasted_iota(jnp.int32, (GLA_W, LANES), 1)
    bdg = (ge // HEAD_DIM) == (gd // GLA_DK)
    q = cols(C_GQ) * (GLA_DK ** -0.5)
    o, st = _gated_chunks(q, cols(C_GK), cols(C_GV, GLA_W), cum, gl_st[...], GLA_DK, bdg, rows)
    gl_st[...] = st
    for gi in range(GLA_W // LANES):
        o0 = gi * LANES
        y = _head_norm_gate(o[:, o0:o0 + LANES], bdmean, glan_ref[:, o0:o0 + LANES], cols(C_GG + o0))
        o_ref[0, :, RET_W + HGRN_W + o0:RET_W + HGRN_W + o0 + LANES] = y.astype(o_ref.dtype)


def _mixer_constants(t):
    idx = np.arange(t)
    tri = ((idx[:, None] >= idx[None, :]) & (idx[:, None] // CHUNK == idx[None, :] // CHUNK))
    log_gamma = jnp.log1p(-jnp.exp2(-5.0 - jnp.arange(RET_HEADS, dtype=F32)))
    i = jnp.arange(t, dtype=F32)
    rel = i[:, None] - i[None, :]
    rdec = jnp.where(rel >= 0, jnp.exp(log_gamma[:, None, None] * jnp.maximum(rel, 0.0)), 0.0)
    lane_head = jnp.repeat(log_gamma, HEAD_DIM).reshape(N_RET_PAIRS, 1, LANES)
    xi = jnp.exp(lane_head * (i[None, :, None] + 1.0))
    zeta = jnp.exp(lane_head * (t - 1.0 - i[None, :, None]))
    rcd = jnp.exp(lane_head * float(t))
    return jnp.asarray(tri, BF16), rdec, xi, zeta, rcd


def _mixer(matmul_form_ok, proj, cos_t, sin_t, consts, retn, hgn, lb, glan, ba, wa2p):
    b, s, _ = proj.shape
    t = MIX_TILE
    tri, rdec, xi, zeta, rcd = consts
    tile = lambda bi, j, ok: (bi, j, 0)
    const2 = lambda bi, j, ok: (0, 0)
    const3 = lambda bi, j, ok: (0, 0, 0)
    return pl.pallas_call(
        _mixer_kernel,
        grid_spec=pltpu.PrefetchScalarGridSpec(
            num_scalar_prefetch=1,
            grid=(b, s // t),
            in_specs=[pl.BlockSpec((1, t, PROJ_W), tile),
                      pl.BlockSpec((1, t, LANES), tile),
                      pl.BlockSpec((1, t, LANES), tile),
                      pl.BlockSpec((t, t), const2),
                      pl.BlockSpec((RET_HEADS, t, t), const3),
                      pl.BlockSpec((N_RET_PAIRS, t, LANES), const3),
                      pl.BlockSpec((N_RET_PAIRS, t, LANES), const3),
                      pl.BlockSpec((N_RET_PAIRS, 1, LANES), const3),
                      pl.BlockSpec((1, RET_W), const2),
                      pl.BlockSpec((1, HGRN_W), const2),
                      pl.BlockSpec((1, HGRN_W), const2),
                      pl.BlockSpec((1, GLA_W), const2),
                      pl.BlockSpec((1, GLA_QK), const2),
                      pl.BlockSpec((LANES, GLA_QK), const2)],
            out_specs=pl.BlockSpec((1, t, D_MODEL), tile),
            scratch_shapes=[pltpu.VMEM((N_RET_PAIRS, LANES, LANES), F32),
                            pltpu.VMEM((N_HGRN_PAIRS, LANES, LANES), F32),
                            pltpu.VMEM((GLA_W, LANES), F32),
                            pltpu.VMEM((t, LANES), F32), pltpu.VMEM((t, LANES), F32),
                            pltpu.VMEM((t, LANES), F32), pltpu.VMEM((t, GLA_W), F32),
                            pltpu.VMEM((t, GLA_W), F32)]),
        out_shape=jax.ShapeDtypeStruct((b, s, D_MODEL), BF16),
        compiler_params=_params("arbitrary", "arbitrary"),
        name="mixer",
    )(matmul_form_ok, proj, cos_t, sin_t, tri, rdec, xi, zeta, rcd, retn, hgn, lb, glan, ba, wa2p)


def _route(logits):
    lane = lax.broadcasted_iota(jnp.int32, logits.shape, 1)
    neg = jnp.float32(-jnp.inf)
    big = jnp.int32(LANES)
    is_g = (lane >= N_EXPERTS) & (lane < N_EXPERTS + N_GROUPS)
    gl = jnp.where(is_g, logits, neg)
    gmax = jnp.max(gl, axis=-1, keepdims=True)
    gidx = jnp.min(jnp.where(gl == gmax, lane - N_EXPERTS, big), axis=-1, keepdims=True)
    g_w = 1.0 / jnp.sum(jnp.where(is_g, jnp.exp(gl - gmax), 0.0), axis=-1, keepdims=True)
    in_grp = (lane < N_EXPERTS) & ((lane // EXPERTS_PER_GROUP) == gidx)
    el = jnp.where(in_grp, logits, neg)
    v1 = jnp.max(el, axis=-1, keepdims=True)
    i1 = jnp.min(jnp.where(in_grp & (el == v1), lane, big), axis=-1, keepdims=True)
    rest = in_grp & (lane != i1)
    el2 = jnp.where(rest, logits, neg)
    v2 = jnp.max(el2, axis=-1, keepdims=True)
    i2 = jnp.min(jnp.where(rest & (el2 == v2), lane, big), axis=-1, keepdims=True)
    e2 = jnp.exp(v2 - v1)
    w1 = g_w / (1.0 + e2)
    w2 = g_w * e2 / (1.0 + e2)
    return i1, i2, w1, w2


R_E1, R_E2, R_RANK1, R_RANK2, R_W1, R_W2 = 0, 1, 2, 3, 4, 5


def _outproj_kernel(m_ref, x_ref, mod_ref, g_ref, wo_ref, wr_ref, br_ref, tril_ref,
                    x1_ref, h2_ref, rinfo_ref, rfields_ref, counts_ref, carry_ref):
    @pl.when(pl.program_id(0) == 0)
    def _():
        carry_ref[...] = jnp.zeros_like(carry_ref)

    att = jnp.dot(m_ref[...], wo_ref[...], preferred_element_type=F32)
    x1 = x_ref[...] + mod_ref[0, 2:3, :] * att
    x1_ref[...] = x1
    h2 = _modulated_norm(x1, g_ref[...], mod_ref[0, 3:4, :], mod_ref[0, 4:5, :])
    _to_slabs(h2_ref, h2)
    h_hi = h2.astype(BF16)
    h_lo = (h2 - h_hi.astype(F32)).astype(BF16)
    both = jnp.dot(h_hi, wr_ref[...], preferred_element_type=F32)
    cross = jnp.dot(h_lo, wr_ref[:, :LANES], preferred_element_type=F32)
    logits = both[:, :LANES] + both[:, LANES:] + cross + br_ref[...]
    i1, i2, w1, w2 = _route(logits)
    lane = lax.broadcasted_iota(jnp.int32, logits.shape, 1)
    hit1 = lane == i1
    hit2 = lane == i2
    onehot = jnp.where(hit1 | hit2, 1.0, 0.0)
    before = carry_ref[...] + jnp.dot(tril_ref[...], onehot.astype(BF16), preferred_element_type=F32)
    r1 = jnp.sum(jnp.where(hit1, before, 0.0), axis=-1, keepdims=True)
    r2 = jnp.sum(jnp.where(hit2, before, 0.0), axis=-1, keepdims=True)
    carry_ref[...] += jnp.sum(onehot, axis=0, keepdims=True)
    counts_ref[...] = carry_ref[...]
    rec = jnp.zeros(logits.shape, F32)
    for pos, val in ((R_E1, i1.astype(F32)), (R_E2, i2.astype(F32)), (R_RANK1, r1), (R_RANK2, r2),
                     (R_W1, w1), (R_W2, w2)):
        rec = jnp.where(lane == pos, val, rec)
    rinfo_ref[...] = rec
    rfields_ref[...] = rec.T[:SUBLANES]


def _outproj(merged2d, x2d, mod, gain, w_out, w_r, b_r, seq):
    n, d = x2d.shape
    tm = ROW_TILE
    per_b = seq // tm
    row = lambda i: (i, 0)
    const2 = lambda i: (0, 0)
    idx = np.arange(tm)
    tril = jnp.asarray(idx[:, None] > idx[None, :], BF16)
    return pl.pallas_call(
        _outproj_kernel,
        grid=(n // tm,),
        in_specs=[pl.BlockSpec((tm, d), row),
                  pl.BlockSpec((tm, d), row),
                  pl.BlockSpec((1, N_MOD, d), lambda i: (i // per_b, 0, 0)),
                  pl.BlockSpec((1, d), const2),
                  pl.BlockSpec((d, d), const2),
                  pl.BlockSpec((d, 2 * LANES), const2),
                  pl.BlockSpec((1, LANES), const2),
                  pl.BlockSpec((tm, tm), const2)],
        out_specs=[pl.BlockSpec((tm, d), row),
                   pl.BlockSpec((tm * F32_CHUNKS, LANES), row),
                   pl.BlockSpec((tm, LANES), row),
                   pl.BlockSpec((SUBLANES, tm), lambda i: (0, i)),
                   pl.BlockSpec((1, LANES), const2)],
        out_shape=[jax.ShapeDtypeStruct((n, d), F32),
                   jax.ShapeDtypeStruct((n * F32_CHUNKS, LANES), F32),
                   jax.ShapeDtypeStruct((n, LANES), F32),
                   jax.ShapeDtypeStruct((SUBLANES, n), F32),
                   jax.ShapeDtypeStruct((1, LANES), F32)],
        scratch_shapes=[pltpu.VMEM((1, LANES), F32)],
        compiler_params=_params("arbitrary"),
        name="outproj_route",
    )(merged2d, x2d, mod, gain, w_out, w_r, b_r, tril)


def _slot_layout(rfields, counts, n_tiles):
    cnt = counts[0, :N_EXPERTS].astype(jnp.int32)
    padded = (cnt + EXPERT_TILE - 1) // EXPERT_TILE * EXPERT_TILE
    ends = jnp.cumsum(padded)
    starts = ends - padded
    e = rfields[R_E1:R_E2 + 1].astype(jnp.int32)
    rank = rfields[R_RANK1:R_RANK2 + 1].astype(jnp.int32)
    ids = jnp.arange(N_EXPERTS, dtype=jnp.int32)[:, None, None]
    start_of = jnp.sum(jnp.where(e[None] == ids, starts[:, None, None], 0), axis=0)
    pos = (start_of + rank).reshape(-1)
    tile_ids = jnp.arange(n_tiles, dtype=jnp.int32)
    tile_start = tile_ids * EXPERT_TILE
    tile_expert = jnp.sum((tile_start[:, None] >= ends[None, :]).astype(jnp.int32), axis=-1)
    tile_expert = jnp.minimum(tile_expert, N_EXPERTS - 1)
    tile_valid = (tile_start < ends[-1]).astype(jnp.int32)
    used_tiles = ends[-1:] // EXPERT_TILE
    tile_src = jnp.minimum(tile_ids, used_tiles - 1)
    return pos, starts + cnt, padded - cnt, used_tiles, tile_expert, tile_valid, tile_src


def _row_copy(src_ref, src_tok, dst_ref, dst_tok, chunks, sem):
    src = src_ref.at[pl.ds(pl.multiple_of(src_tok * chunks, chunks), chunks)]
    dst = dst_ref.at[pl.ds(pl.multiple_of(dst_tok * chunks, chunks), chunks)]
    return pltpu.make_async_copy(src, dst, sem)


def _to_slabs(ref, val):
    t = val.shape[0]
    chunks = ref.shape[0] // t
    for j in range(chunks):
        ref[pl.ds(j, t, stride=chunks), :] = val[:, j * LANES:(j + 1) * LANES]


def _from_slabs(ref, chunks):
    t = ref.shape[0] // chunks
    return jnp.concatenate([ref[pl.ds(j, t, stride=chunks), :] for j in range(chunks)], axis=1)


def _pad_copy(zbuf, xs_ref, pad_start, pad_len, bit, sem):
    rows = (1 << bit) * F32_CHUNKS
    slot = pad_start + (pad_len & ((1 << bit) - 1))
    dst = xs_ref.at[pl.ds(pl.multiple_of(slot * F32_CHUNKS, F32_CHUNKS), rows)]
    return pltpu.make_async_copy(zbuf.at[pl.ds(0, rows)], dst, sem)


def _dispatch_kernel(pos_ref, pstart_ref, plen_ref, used_ref, h_ref, xs_ref, zbuf, sem, zsem):
    @pl.when(pl.program_id(0) == 0)
    def _():
        zbuf[...] = jnp.zeros_like(zbuf)
        tile_rows = EXPERT_TILE * F32_CHUNKS

        def each(e, carry, wait):
            for bit in range(PAD_BITS):
                @pl.when(((plen_ref[e] >> bit) & 1) == 1)
                def _():
                    copy = _pad_copy(zbuf, xs_ref, pstart_ref[e], plen_ref[e], bit, zsem)
                    copy.wait() if wait else copy.start()
            return carry

        def tail(j, carry, wait):
            dst = xs_ref.at[pl.ds(pl.multiple_of(j * tile_rows, tile_rows), tile_rows)]
            copy = pltpu.make_async_copy(zbuf, dst, zsem)
            copy.wait() if wait else copy.start()
            return carry

        n_tiles = xs_ref.shape[0] // tile_rows
        for wait in (False, True):
            lax.fori_loop(0, N_EXPERTS, functools.partial(each, wait=wait), 0)
            lax.fori_loop(used_ref[0], n_tiles, functools.partial(tail, wait=wait), 0)

    td = h_ref.shape[0] // F32_CHUNKS
    base = pl.program_id(0) * td
    n_tok = pos_ref.shape[0] // 2

    def issue(t, carry):
        for k in range(2):
            _row_copy(h_ref, t, xs_ref, pos_ref[k * n_tok + base + t], F32_CHUNKS, sem).start(priority=k)
        return carry

    lax.fori_loop(0, td, issue, 0, unroll=8)

    def drain(t, carry):
        for k in range(2):
            _row_copy(h_ref, 0, xs_ref, 0, F32_CHUNKS, sem).wait()
        return carry

    lax.fori_loop(0, td, drain, 0, unroll=8)


def _dispatch(pos, pad_start, pad_len, used_tiles, h2, n_slots):
    n = h2.shape[0] // F32_CHUNKS
    td = min(DISPATCH_TILE, n)
    return pl.pallas_call(
        _dispatch_kernel,
        grid_spec=pltpu.PrefetchScalarGridSpec(
            num_scalar_prefetch=4,
            grid=(n // td,),
            in_specs=[pl.BlockSpec((td * F32_CHUNKS, LANES), lambda i, *_: (i, 0))],
            out_specs=pl.BlockSpec(memory_space=pl.ANY),
            scratch_shapes=[pltpu.VMEM((EXPERT_TILE * F32_CHUNKS, LANES), F32),
                            pltpu.SemaphoreType.DMA(()),
                            pltpu.SemaphoreType.DMA(())]),
        out_shape=jax.ShapeDtypeStruct((n_slots * F32_CHUNKS, LANES), h2.dtype),
        compiler_params=_params("arbitrary"),
        name="moe_dispatch",
    )(pos, pad_start, pad_len, used_tiles, h2)


def _expert_kernel(te_ref, tv_ref, ts_ref, xs_ref, wg_ref, wu_ref, wd_ref, ys_ref, wgb, wub, wdb):
    del ts_ref
    i = pl.program_id(0)
    new_expert = jnp.logical_or(i == 0, te_ref[i] != te_ref[jnp.maximum(i - 1, 0)])

    @pl.when(new_expert)
    def _():
        wgb[...] = wg_ref[0, 0].astype(BF16)
        wub[...] = wu_ref[0, 0].astype(BF16)
        wdb[...] = wd_ref[0, 0].astype(BF16)

    @pl.when(tv_ref[i] != 0)
    def _():
        x = _from_slabs(xs_ref, F32_CHUNKS).astype(BF16)
        a = jnp.dot(x, wgb[...], preferred_element_type=F32)
        b = jnp.dot(x, wub[...], preferred_element_type=F32)
        _to_slabs(ys_ref, jnp.dot((_silu(a) * b).astype(BF16), wdb[...], preferred_element_type=F32))

    @pl.when(tv_ref[i] == 0)
    def _():
        ys_ref[...] = jnp.zeros_like(ys_ref)


def _experts(tile_expert, tile_valid, tile_src, xs, wg, wu, wd, layer):
    n_slots = xs.shape[0] // F32_CHUNKS
    d = D_MODEL
    tm = EXPERT_TILE
    by_expert = lambda i, te, tv, ts: (layer, te[i], 0, 0)
    return pl.pallas_call(
        _expert_kernel,
        grid_spec=pltpu.PrefetchScalarGridSpec(
            num_scalar_prefetch=3,
            grid=(n_slots // tm,),
            in_specs=[pl.BlockSpec((tm * F32_CHUNKS, LANES), lambda i, te, tv, ts: (ts[i], 0)),
                      pl.BlockSpec((1, 1, d, D_EXPERT), by_expert),
                      pl.BlockSpec((1, 1, d, D_EXPERT), by_expert),
                      pl.BlockSpec((1, 1, D_EXPERT, d), by_expert)],
            out_specs=pl.BlockSpec((tm * F32_CHUNKS, LANES), lambda i, te, tv, ts: (i, 0)),
            scratch_shapes=[pltpu.VMEM((d, D_EXPERT), BF16),
                            pltpu.VMEM((d, D_EXPERT), BF16),
                            pltpu.VMEM((D_EXPERT, d), BF16)]),
        out_shape=jax.ShapeDtypeStruct((n_slots * F32_CHUNKS, LANES), F32),
        compiler_params=_params("arbitrary"),
        name="moe_experts",
    )(tile_expert, tile_valid, tile_src, xs, wg, wu, wd)


def _combine_kernel(pos_ref, x1_ref, rinfo_ref, mod_ref, nf_ref, ys_ref, o_ref, buf, sem, *, final):
    i = pl.program_id(0)
    tc = x1_ref.shape[0]
    n_tok = pos_ref.shape[0] // 2

    def issue(tile, slot):
        base = tile * tc

        def body(t, carry):
            for k in range(2):
                _row_copy(ys_ref, pos_ref[k * n_tok + base + t], buf.at[slot, k], t, F32_CHUNKS,
                          sem.at[slot]).start(priority=k)
            return carry

        lax.fori_loop(0, tc, body, 0, unroll=8)

    @pl.when(i == 0)
    def _():
        issue(0, 0)

    @pl.when(i + 1 < pl.num_programs(0))
    def _():
        issue(i + 1, (i + 1) % 2)

    slot = i % 2

    def drain(t, carry):
        for k in range(2):
            _row_copy(ys_ref, 0, buf.at[slot, k], 0, F32_CHUNKS, sem.at[slot]).wait()
        return carry

    lax.fori_loop(0, tc, drain, 0, unroll=8)

    rinfo = rinfo_ref[...]
    y = (rinfo[:, R_W1:R_W1 + 1] * _from_slabs(buf.at[slot, 0], F32_CHUNKS)
         + rinfo[:, R_W2:R_W2 + 1] * _from_slabs(buf.at[slot, 1], F32_CHUNKS))
    x2 = x1_ref[...] + mod_ref[0, 5:6, :] * y
    if final:
        ms = jnp.mean(x2 * x2, axis=-1, keepdims=True)
        x2 = x2 * lax.rsqrt(ms + EPS) * nf_ref[...]
    o_ref[...] = x2


def _combine(pos, x1, rinfo, mod, nf, ys, seq, final):
    n, d = x1.shape
    tc = COMBINE_TILE
    per_b = seq // tc
    row = lambda i, pos: (i, 0)
    return pl.pallas_call(
        functools.partial(_combine_kernel, final=final),
        grid_spec=pltpu.PrefetchScalarGridSpec(
            num_scalar_prefetch=1,
            grid=(n // tc,),
            in_specs=[pl.BlockSpec((tc, d), row),
                      pl.BlockSpec((tc, LANES), row),
                      pl.BlockSpec((1, N_MOD, d), lambda i, pos: (i // per_b, 0, 0)),
                      pl.BlockSpec((1, d), lambda i, pos: (0, 0)),
                      pl.BlockSpec(memory_space=pl.ANY)],
            out_specs=pl.BlockSpec((tc, d), row),
            scratch_shapes=[pltpu.VMEM((2, 2, tc * F32_CHUNKS, LANES), F32),
                            pltpu.SemaphoreType.DMA((2,))]),
        out_shape=jax.ShapeDtypeStruct((n, d), F32),
        compiler_params=_params("arbitrary"),
        name="moe_combine",
    )(pos, x1, rinfo, mod, nf, ys)


def kernel(x, c, positions, w_ada, b_ada, norm_mix, norm_ffn, w_in, ret_norm, hgrn_norm, hgrn_lb_logits,
           gla_wa2, gla_ba, gla_norm, w_out, router_group_w, router_group_b, router_expert_w,
           router_expert_b, expert_w_gate, expert_w_up, expert_w_down, norm_final):
    b, s, d = x.shape
    depth = w_ada.shape[0]
    n = b * s
    assert d == D_MODEL and s % ROW_TILE == 0 and s % MIX_TILE == 0

    inv_freq = ROPE_BASE ** (-jnp.arange(0, HEAD_DIM, 2, dtype=F32) / HEAD_DIM)
    half = HEAD_DIM // 2
    ang = positions.astype(F32)[..., None] * jnp.tile(inv_freq, LANES // half)
    sign = jnp.where((jnp.arange(LANES) // half) % 2 == 0, -1.0, 1.0).astype(F32)
    cos_t = jnp.cos(ang)
    sin_t = jnp.sin(ang) * sign

    lb_w = jax.nn.softmax(hgrn_lb_logits.astype(F32), axis=0)
    lower_bounds = jnp.cumsum(lb_w, axis=0) - lb_w[0]

    consts = _mixer_constants(MIX_TILE)
    mod_all = _ada_mod(c, w_ada, b_ada).reshape(depth, b, N_MOD, d)

    w_in_pad = jnp.pad(w_in, ((0, 0), (0, 0), (0, PROJ_W - IN_PROJ_DIM))).astype(BF16)
    wa2_pad = jnp.pad(gla_wa2, ((0, 0), (0, LANES - GLA_RANK), (0, 0)))
    w_out_b = w_out.astype(BF16)
    pad_r = LANES - N_EXPERTS - N_GROUPS
    w_r = jnp.pad(jnp.concatenate([router_expert_w, router_group_w], axis=-1), ((0, 0), (0, 0), (0, pad_r)))
    w_r_hi = w_r.astype(BF16)
    w_r = jnp.concatenate([w_r_hi, (w_r - w_r_hi.astype(F32)).astype(BF16)], axis=-1)
    b_r = jnp.pad(jnp.concatenate([router_expert_b, router_group_b], axis=-1), ((0, 0), (0, pad_r)))
    n_slots = (2 * n // EXPERT_TILE + N_EXPERTS) * EXPERT_TILE

    x2d = x.reshape(n, d)
    for l in range(depth):
        mod = mod_all[l]
        lb_l = lower_bounds[l].reshape(1, HGRN_W)
        ba_l = gla_ba[l].reshape(1, GLA_QK)
        proj, sub_decay = _inproj(x2d, mod, norm_mix[l].reshape(1, d), w_in_pad[l], lb_l, wa2_pad[l], ba_l, s)
        matmul_form_ok = (sub_decay[:, 0, :ROW_TILE // MIX_TILE].reshape(-1) > -MAX_SUB_DECAY).astype(jnp.int32)
        merged = _mixer(matmul_form_ok, proj.reshape(b, s, PROJ_W), cos_t, sin_t, consts,
                        ret_norm[l].reshape(1, RET_W), hgrn_norm[l].reshape(1, HGRN_W),
                        lower_bounds[l].reshape(1, HGRN_W), gla_norm[l].reshape(1, GLA_W),
                        gla_ba[l].reshape(1, GLA_QK), wa2_pad[l])
        x1, h2p, rinfo, rfields, counts = _outproj(merged.reshape(n, d), x2d, mod, norm_ffn[l].reshape(1, d),
                                                   w_out_b[l], w_r[l], b_r[l].reshape(1, LANES), s)
        pos, pad_start, pad_len, used_tiles, tile_expert, tile_valid, tile_src = _slot_layout(
            rfields, counts, n_slots // EXPERT_TILE)
        xs = _dispatch(pos, pad_start, pad_len, used_tiles, h2p, n_slots)
        ys = _experts(tile_expert, tile_valid, tile_src, xs, expert_w_gate, expert_w_up, expert_w_down, l)
        x2d = _combine(pos, x1, rinfo, mod, norm_final.reshape(1, d), ys, s, final=(l == depth - 1))
    return x2d.reshape(b, s, d)
```
